```python
import jax, jax.numpy as jnp
from jax import lax
import numpy as np

D_MODEL = 1024
BATCH = 4
SEQ = 4096
DEPTH = 2

GRID_W = 64
NA_WIN_ROWS = 8
NA_WIN_COLS = 16
NA_HEADS = 16
NA_HEAD_DIM = D_MODEL // NA_HEADS
MLA_HEADS = 16
MLA_Q_RANK = 384
MLA_KV_RANK = 256
MLA_NOPE = 64
MLA_ROPE = 32
MLA_V = 64
ROPE_BASE = 10000.0
Q_BLOCK = 128
PLE_DIM = 256
EPS = 1e-6
N_MIXERS = 2
N_A = (DEPTH + 1) // 2
N_B = DEPTH // 2

kernel_name = "hybrid_natten_mla_encoder"


def rmsnorm(x, g):
    xf = x.astype(jnp.float32)
    y = xf * lax.rsqrt(jnp.mean(xf * xf, axis=-1, keepdims=True) + EPS)
    return (y * g.astype(jnp.float32)).astype(x.dtype)


def rope_tables(seq, dim):
    inv = 1.0 / (ROPE_BASE ** (jnp.arange(0, dim, 2, dtype=jnp.float32) / dim))
    ang = jnp.arange(seq, dtype=jnp.float32)[:, None] * inv[None, :]
    return jnp.cos(ang), jnp.sin(ang)


def apply_rope(x, cos, sin):
    x1, x2 = jnp.split(x, 2, axis=-1)
    c = cos[None, :, None, :].astype(x.dtype)
    s = sin[None, :, None, :].astype(x.dtype)
    return jnp.concatenate([x1 * c - x2 * s, x1 * s + x2 * c], axis=-1)


def neighbourhood_attention(q, k, v, rpb):
    B, S, H, dh = q.shape
    rows = S // GRID_W
    kr = min(NA_WIN_ROWS, rows)
    kc = NA_WIN_COLS
    qg = q.reshape(B, rows, GRID_W, H, dh)
    kg = k.reshape(B, rows, GRID_W, H, dh)
    vg = v.reshape(B, rows, GRID_W, H, dh)
    cols = jnp.arange(GRID_W)
    col_start = jnp.clip(cols - kc // 2, 0, GRID_W - kc)
    col_idx = col_start[:, None] + jnp.arange(kc)[None, :]
    col_off = col_idx - cols[:, None] + (NA_WIN_COLS - 1)
    scale = dh ** -0.5

    def row_block(r):
        rs = jnp.clip(r - kr // 2, 0, rows - kr)
        q_r = lax.dynamic_index_in_dim(qg, r, axis=1, keepdims=False)
        k_band = lax.dynamic_slice_in_dim(kg, rs, kr, axis=1)
        v_band = lax.dynamic_slice_in_dim(vg, rs, kr, axis=1)
        k_win = k_band[:, :, col_idx]
        v_win = v_band[:, :, col_idx]
        row_off = rs + jnp.arange(kr) - r + (NA_WIN_ROWS - 1)
        bias = rpb[:, row_off[:, None, None], col_off[None, :, :]]
        bias = bias.transpose(0, 2, 1, 3).astype(jnp.float32)
        s = jnp.einsum('bwhd,biwjhd->bhwij', q_r, k_win).astype(jnp.float32) * scale
        s = (s + bias[None]).reshape(B, H, GRID_W, kr * kc)
        pr = jax.nn.softmax(s, axis=-1).reshape(B, H, GRID_W, kr, kc).astype(v.dtype)
        return jnp.einsum('bhwij,biwjhd->bwhd', pr, v_win)

    o = lax.map(row_block, jnp.arange(rows))
    return o.transpose(1, 0, 2, 3, 4).reshape(B, S, H, dh)


def block_attention(q, k, v):
    B, S, H, dqk = q.shape
    nb = S // Q_BLOCK
    qb = q.reshape(B, nb, Q_BLOCK, H, dqk).transpose(1, 0, 2, 3, 4)
    scale = dqk ** -0.5

    def one(qblk):
        s = jnp.einsum('bqhd,bkhd->bhqk', qblk, k).astype(jnp.float32) * scale
        pr = jax.nn.softmax(s, axis=-1).astype(v.dtype)
        return jnp.einsum('bhqk,bkhd->bqhd', pr, v)

    o = lax.map(one, qb)
    return o.transpose(1, 0, 2, 3, 4).reshape(B, S, H, v.shape[-1])


def mixer_na(xn, w_in, rpb, w_out):
    B, S, _ = xn.shape
    hd = NA_HEADS * NA_HEAD_DIM
    q, k, v, z = jnp.split(xn @ w_in, 4, axis=-1)
    shp = (B, S, NA_HEADS, NA_HEAD_DIM)
    o = neighbourhood_attention(q.reshape(shp), k.reshape(shp), v.reshape(shp), rpb)
    return (o.reshape(B, S, hd) * jax.nn.silu(z)) @ w_out


def mixer_mla(xn, w_in, q_norm, w_qb, kv_norm, w_kvb, w_out):
    B, S, _ = xn.shape
    H = MLA_HEADS
    c_q, c_kv, k_rope, z = jnp.split(
        xn @ w_in, [MLA_Q_RANK, MLA_Q_RANK + MLA_KV_RANK, MLA_Q_RANK + MLA_KV_RANK + MLA_ROPE], axis=-1)
    cos, sin = rope_tables(S, MLA_ROPE)
    q = (rmsnorm(c_q, q_norm) @ w_qb).reshape(B, S, H, MLA_NOPE + MLA_ROPE)
    q_nope, q_pe = jnp.split(q, [MLA_NOPE], axis=-1)
    q_pe = apply_rope(q_pe, cos, sin)
    kv = (rmsnorm(c_kv, kv_norm) @ w_kvb).reshape(B, S, H, MLA_NOPE + MLA_V)
    k_nope, v = jnp.split(kv, [MLA_NOPE], axis=-1)
    k_pe = apply_rope(k_rope[:, :, None, :], cos, sin)
    k = jnp.concatenate([k_nope, jnp.broadcast_to(k_pe, (B, S, H, MLA_ROPE))], axis=-1)
    q = jnp.concatenate([q_nope, q_pe], axis=-1)
    o = block_attention(q, k, v)
    return (o.reshape(B, S, H * MLA_V) * jax.nn.silu(z)) @ w_out


def setup_inputs(seed: int = 0) -> dict:
    key = jax.random.key(seed)
    ks = jax.random.split(key, 20)
    f32 = jnp.float32

    def w(k, shape, fan_in):
        return jax.random.normal(k, shape, f32) * fan_in ** -0.5

    def gain(k, shape):
        return 1.0 + 0.05 * jax.random.normal(k, shape, f32)

    na_in = 4 * NA_HEADS * NA_HEAD_DIM
    mla_in = MLA_Q_RANK + MLA_KV_RANK + MLA_ROPE + MLA_HEADS * MLA_V
    return {
        "x": jax.random.normal(ks[0], (BATCH, SEQ, D_MODEL), f32),
        "p": jax.random.normal(ks[1], (DEPTH, BATCH, SEQ, PLE_DIM), f32),
        "norm_g": gain(ks[2], (DEPTH, D_MODEL)),
        "na_w_in": w(ks[3], (N_A, D_MODEL, na_in), D_MODEL),
        "na_rpb": 0.1 * jax.random.normal(ks[4], (N_A, NA_HEADS, 2 * NA_WIN_ROWS - 1, 2 * NA_WIN_COLS - 1), f32),
        "na_w_out": w(ks[5], (N_A, NA_HEADS * NA_HEAD_DIM, D_MODEL), NA_HEADS * NA_HEAD_DIM),
        "mla_w_in": w(ks[6], (N_B, D_MODEL, mla_in), D_MODEL),
        "mla_q_norm": gain(ks[7], (N_B, MLA_Q_RANK)),
        "mla_w_qb": w(ks[8], (N_B, MLA_Q_RANK, MLA_HEADS * (MLA_NOPE + MLA_ROPE)), MLA_Q_RANK),
        "mla_kv_norm": gain(ks[9], (N_B, MLA_KV_RANK)),
        "mla_w_kvb": w(ks[10], (N_B, MLA_KV_RANK, MLA_HEADS * (MLA_NOPE + MLA_V)), MLA_KV_RANK),
        "mla_w_out": w(ks[11], (N_B, MLA_HEADS * MLA_V, D_MODEL), MLA_HEADS * MLA_V),
        "ple_norm": gain(ks[12], (DEPTH, D_MODEL)),
        "ple_w_gate": w(ks[13], (DEPTH, D_MODEL, D_MODEL), D_MODEL),
        "ple_w_proj": w(ks[14], (DEPTH, PLE_DIM, D_MODEL), PLE_DIM),
        "final_norm": gain(ks[15], (D_MODEL,)),
    }


def reference(x, p, norm_g, na_w_in, na_rpb, na_w_out, mla_w_in, mla_q_norm, mla_w_qb,
              mla_kv_norm, mla_w_kvb, mla_w_out, ple_norm, ple_w_gate, ple_w_proj, final_norm):
    for i in range(DEPTH):
        xn = rmsnorm(x, norm_g[i])
        j = i // N_MIXERS
        if i % N_MIXERS == 0:
            y = mixer_na(xn, na_w_in[j], na_rpb[j], na_w_out[j])
        else:
            y = mixer_mla(xn, mla_w_in[j], mla_q_norm[j], mla_w_qb[j],
                          mla_kv_norm[j], mla_w_kvb[j], mla_w_out[j])
        h = x + y
        gate = jax.nn.sigmoid(rmsnorm(h, ple_norm[i]) @ ple_w_gate[i])
        x = h + gate * (p[i] @ ple_w_proj[i])
    return rmsnorm(x, final_norm)
```

```python
import functools

import jax
import jax.numpy as jnp
from jax import lax
from jax.experimental import pallas as pl
from jax.experimental.pallas import tpu as pltpu

D_MODEL = 1024
GRID_W = 64
NA_WIN_ROWS = 8
NA_WIN_COLS = 16
NA_HEADS = 16
NA_HEAD_DIM = 64
MLA_HEADS = 16
MLA_Q_RANK = 384
MLA_KV_RANK = 256
MLA_NOPE = 64
MLA_ROPE = 32
MLA_V = 64
ROPE_BASE = 10000.0
PLE_DIM = 256
EPS = 1e-6

LANES = 128
NEG_BIG = -1e30
VMEM_LIMIT = 56 * 1024 * 1024

TM_PROJ = 512
MLA_BQ = 512
MLA_BK = 256

BF16 = jnp.bfloat16
F32 = jnp.float32


def _rms(x, g):
    ms = jnp.mean(x * x, axis=-1, keepdims=True)
    return x * lax.rsqrt(ms + EPS) * g


def _sigmoid(x):
    return 1.0 / (1.0 + jnp.exp(-x))


def _dot(a, b):
    return jnp.dot(a, b, preferred_element_type=F32)


def _dot_nt(a, b):
    return lax.dot_general(a, b, (((1,), (1,)), ((), ())), preferred_element_type=F32)


def _dot_tn(a, b):
    return lax.dot_general(a, b, (((0,), (0,)), ((), ())), preferred_element_type=F32)


def _na_inproj_kernel(x_ref, g_ref, w_ref, o_ref, xn_ref):
    @pl.when(pl.program_id(1) == 0)
    def _():
        xn_ref[...] = _rms(x_ref[...], g_ref[...]).astype(BF16)

    o_ref[...] = _dot(xn_ref[...], w_ref[...]).astype(BF16)


def _na_inproj(x2, g, w):
    m = x2.shape[0]
    n = w.shape[1]
    tn = 1024
    return pl.pallas_call(
        _na_inproj_kernel,
        grid=(m // TM_PROJ, n // tn),
        in_specs=[
            pl.BlockSpec((TM_PROJ, D_MODEL), lambda i, j: (i, 0)),
            pl.BlockSpec((1, D_MODEL), lambda i, j: (0, 0)),
            pl.BlockSpec((D_MODEL, tn), lambda i, j: (0, j)),
        ],
        out_specs=pl.BlockSpec((TM_PROJ, tn), lambda i, j: (i, j)),
        out_shape=jax.ShapeDtypeStruct((m, n), BF16),
        scratch_shapes=[pltpu.VMEM((TM_PROJ, D_MODEL), BF16)],
        compiler_params=pltpu.CompilerParams(
            dimension_semantics=("arbitrary", "arbitrary"),
            vmem_limit_bytes=VMEM_LIMIT),
        name="na_inproj",
    )(x2, g, w)


def _na_bias_table(rpb):
    rows = 2 * NA_WIN_ROWS - 1
    t = jnp.arange(NA_WIN_ROWS)
    i = jnp.arange(NA_WIN_ROWS)
    row_off = i[None, :] - t[:, None] + (NA_WIN_ROWS - 1)
    c = jnp.arange(GRID_W)
    j = jnp.arange(GRID_W)
    cs = jnp.clip(c - NA_WIN_COLS // 2, 0, GRID_W - NA_WIN_COLS)
    valid = (j[None, :] >= cs[:, None]) & (j[None, :] < cs[:, None] + NA_WIN_COLS)
    col_off = jnp.clip(j[None, :] - c[:, None] + (NA_WIN_COLS - 1), 0, 2 * NA_WIN_COLS - 2)
    b = rpb[:, row_off[:, :, None, None], col_off[None, None, :, :]]
    b = jnp.where(valid[None, None, None], b, NEG_BIG)
    b = b.transpose(1, 0, 3, 2, 4)
    del rows
    return b.reshape(NA_WIN_ROWS, NA_HEADS, GRID_W, NA_WIN_ROWS * GRID_W).astype(F32)


def _na_attn_kernel(q_ref, k_ref, v_ref, b_ref, o_ref):
    r = pl.program_id(1)
    rows = k_ref.shape[0] // GRID_W
    rs = jnp.clip(r - NA_WIN_ROWS // 2, 0, rows - NA_WIN_ROWS)
    start = pl.multiple_of(rs * GRID_W, GRID_W)
    band = NA_WIN_ROWS * GRID_W
    lane = lax.broadcasted_iota(jnp.int32, (1, LANES), 1)
    first = lane < NA_HEAD_DIM
    for slab in range(NA_HEADS * NA_HEAD_DIM // LANES):
        cols = slice(slab * LANES, (slab + 1) * LANES)
        qs = q_ref[:, cols]
        kb = k_ref[pl.ds(start, band), cols]
        vb = v_ref[pl.ds(start, band), cols]
        outs = []
        for par in range(2):
            qm = jnp.where(first if par == 0 else jnp.logical_not(first), qs, jnp.zeros_like(qs))
            s = _dot_nt(qm, kb) + b_ref[0, 2 * slab + par]
            m = jnp.max(s, axis=-1, keepdims=True)
            p = jnp.exp(s - m)
            l = jnp.sum(p, axis=-1, keepdims=True)
            outs.append(_dot(p.astype(BF16), vb) / l)
        o_ref[:, cols] = jnp.where(first, outs[0], outs[1]).astype(BF16)


def _na_attn(qkvz, bias, batch, seq):
    hd = NA_HEADS * NA_HEAD_DIM
    rows = seq // GRID_W

    def bias_idx(b, r):
        rs = jnp.clip(r - NA_WIN_ROWS // 2, 0, rows - NA_WIN_ROWS)
        return (r - rs, 0, 0, 0)

    return pl.pallas_call(
        _na_attn_kernel,
        grid=(batch, rows),
        in_specs=[
            pl.BlockSpec((GRID_W, hd), lambda b, r: (b * rows + r, 0)),
            pl.BlockSpec((seq, hd), lambda b, r: (b, 1)),
            pl.BlockSpec((seq, hd), lambda b, r: (b, 2)),
            pl.BlockSpec((1, NA_HEADS, GRID_W, NA_WIN_ROWS * GRID_W), bias_idx),
        ],
        out_specs=pl.BlockSpec((GRID_W, hd), lambda b, r: (b * rows + r, 0)),
        out_shape=jax.ShapeDtypeStruct((batch * seq, hd), BF16),
        compiler_params=pltpu.CompilerParams(
            dimension_semantics=("arbitrary", "arbitrary"),
            vmem_limit_bytes=VMEM_LIMIT),
        name="na_attn",
    )(qkvz, qkvz, qkvz, bias)


def _post_kernel(x_ref, o_ref, z_ref, wo_ref, gp_ref, wg_ref, p_ref, wp_ref, gf_ref,
                 out_ref, *, feature_major, final_norm):
    if feature_major:
        z = z_ref[0].astype(F32)
        gated = (o_ref[0].astype(F32) * (z * _sigmoid(z))).astype(BF16)
        y = _dot_tn(gated, wo_ref[...])
    else:
        z = z_ref[...].astype(F32)
        gated = (o_ref[...].astype(F32) * (z * _sigmoid(z))).astype(BF16)
        y = _dot(gated, wo_ref[...])
    h = x_ref[...] + y
    hn = _rms(h, gp_ref[...]).astype(BF16)
    gate = _sigmoid(_dot(hn, wg_ref[...]))
    emb = _dot(p_ref[...].astype(BF16), wp_ref[...])
    xo = h + gate * emb
    if final_norm:
        xo = _rms(xo, gf_ref[...])
    out_ref[...] = xo


def _post_block(x2, o, z, z_col, w_out, g_ple, w_gate, p2, w_proj, g_final, *,
                feature_major, final_norm, seq):
    m = x2.shape[0]
    tm = TM_PROJ
    per_seq = seq // tm
    if feature_major:
        oz_block = (1, D_MODEL, tm)
        o_spec = pl.BlockSpec(oz_block, lambda i: (i // per_seq, 0, i % per_seq))
        z_spec = pl.BlockSpec(oz_block, lambda i: (i // per_seq, 0, i % per_seq))
    else:
        o_spec = pl.BlockSpec((tm, D_MODEL), lambda i: (i, 0))
        z_spec = pl.BlockSpec((tm, D_MODEL), lambda i: (i, z_col))
    full = lambda shape: pl.BlockSpec(shape, lambda i: (0,) * len(shape))
    kern = functools.partial(_post_kernel, feature_major=feature_major, final_norm=final_norm)
    return pl.pallas_call(
        kern,
        grid=(m // tm,),
        in_specs=[
            pl.BlockSpec((tm, D_MODEL), lambda i: (i, 0)),
            o_spec,
            z_spec,
            full((D_MODEL, D_MODEL)),
            full((1, D_MODEL)),
            full((D_MODEL, D_MODEL)),
            pl.BlockSpec((tm, PLE_DIM), lambda i: (i, 0)),
            full((PLE_DIM, D_MODEL)),
            full((1, D_MODEL)),
        ],
        out_specs=pl.BlockSpec((tm, D_MODEL), lambda i: (i, 0)),
        out_shape=jax.ShapeDtypeStruct((m, D_MODEL), F32),
        compiler_params=pltpu.CompilerParams(
            dimension_semantics=("arbitrary",),
            vmem_limit_bytes=VMEM_LIMIT),
        name="post_block_fm" if feature_major else "post_block",
    )(x2, o, z, w_out, g_ple, w_gate, p2, w_proj, g_final)


def _mla_inproj_kernel(x_ref, g_ref, wc_ref, wzt_ref, gq_ref, wq_ref, wqr_ref,
                       gkv_ref, wk_ref, wvt_ref, cq_ref, sq_ref, ck_ref, sk_ref,
                       qn_ref, qpe_ref, kn_ref, kpe_ref, vt_ref, zt_ref):
    xn = _rms(x_ref[...], g_ref[...]).astype(BF16)
    c = _dot(xn, wc_ref[...])
    zt_ref[0] = _dot_nt(wzt_ref[...], xn).astype(BF16)

    cq = _rms(c[:, :MLA_Q_RANK], gq_ref[...]).astype(BF16)
    ckv = _rms(c[:, MLA_Q_RANK:MLA_Q_RANK + MLA_KV_RANK], gkv_ref[...]).astype(BF16)
    off = MLA_Q_RANK + MLA_KV_RANK
    kpe = c[:, off:off + LANES] * ck_ref[...] + c[:, off + LANES:off + 2 * LANES] * sk_ref[...]
    kpe_ref[...] = kpe.astype(BF16)

    q = _dot(cq, wq_ref[...])
    qrot = _dot(cq, wqr_ref[...])
    nope = MLA_HEADS * MLA_NOPE
    qn_ref[...] = q[:, :nope].astype(BF16)
    qpe_ref[...] = (q[:, nope:] * cq_ref[...] + qrot * sq_ref[...]).astype(BF16)

    kn_ref[...] = _dot(ckv, wk_ref[...]).astype(BF16)
    vt = _dot_nt(wvt_ref[...], ckv).astype(BF16)
    tm = vt.shape[1]
    for h in range(MLA_HEADS):
        for s in range(tm // MLA_BK):
            vt_ref[0, h, s] = vt[h * MLA_V:(h + 1) * MLA_V, s * MLA_BK:(s + 1) * MLA_BK]


def _mla_inproj(x2, g, wc, wzt, gq, wq, wqr, gkv, wk, wvt, cosq, sinq, cosk, sink, batch, seq):
    m = x2.shape[0]
    tm = TM_PROJ
    per_seq = seq // tm
    sub = tm // MLA_BK
    full = lambda a: pl.BlockSpec(a.shape, lambda i: (0,) * a.ndim)
    rowblk = lambda n: pl.BlockSpec((tm, n), lambda i: (i, 0))
    tabblk = lambda n: pl.BlockSpec((tm, n), lambda i: (i % per_seq, 0))
    nope = MLA_HEADS * MLA_NOPE
    pe = MLA_HEADS * MLA_ROPE
    return pl.pallas_call(
        _mla_inproj_kernel,
        grid=(m // tm,),
        in_specs=[rowblk(D_MODEL), full(g), full(wc), full(wzt), full(gq), full(wq), full(wqr),
                  full(gkv), full(wk), full(wvt),
                  tabblk(pe), tabblk(pe), tabblk(LANES), tabblk(LANES)],
        out_specs=[
            rowblk(nope), rowblk(pe), rowblk(nope), rowblk(LANES),
            pl.BlockSpec((1, MLA_HEADS, sub, MLA_V, MLA_BK),
                         lambda i: (i // per_seq, 0, i % per_seq, 0, 0)),
            pl.BlockSpec((1, D_MODEL, tm), lambda i: (i // per_seq, 0, i % per_seq)),
        ],
        out_shape=[
            jax.ShapeDtypeStruct((m, nope), BF16),
            jax.ShapeDtypeStruct((m, pe), BF16),
            jax.ShapeDtypeStruct((m, nope), BF16),
            jax.ShapeDtypeStruct((m, LANES), BF16),
            jax.ShapeDtypeStruct((batch, MLA_HEADS, seq // MLA_BK, MLA_V, MLA_BK), BF16),
            jax.ShapeDtypeStruct((batch, D_MODEL, seq), BF16),
        ],
        compiler_params=pltpu.CompilerParams(
            dimension_semantics=("arbitrary",),
            vmem_limit_bytes=VMEM_LIMIT),
        name="mla_inproj",
    )(x2, g, wc, wzt, gq, wq, wqr, gkv, wk, wvt, cosq, sinq, cosk, sink)


def _mla_attn_kernel(qn_ref, qpe_ref, kn_ref, kpe_ref, vt_ref, ot_ref):
    h = pl.program_id(1)
    lane = lax.broadcasted_iota(jnp.int32, (1, LANES), 1)
    qn = qn_ref[...]
    qp = qpe_ref[...]
    qn = jnp.where(lane // MLA_NOPE == h % (LANES // MLA_NOPE), qn, jnp.zeros_like(qn))
    qp = jnp.where(lane // MLA_ROPE == h % (LANES // MLA_ROPE), qp, jnp.zeros_like(qp))
    q = jnp.concatenate([qn, qp], axis=1)
    bq = q.shape[0]
    nkb = kn_ref.shape[0] // MLA_BK

    def body(kb, carry):
        m, l, acc = carry
        off = pl.multiple_of(kb * MLA_BK, MLA_BK)
        k = jnp.concatenate([kn_ref[pl.ds(off, MLA_BK), :], kpe_ref[pl.ds(off, MLA_BK), :]], axis=1)
        st = _dot_nt(k, q)
        m_new = jnp.maximum(m, jnp.max(st, axis=0, keepdims=True))
        p = jnp.exp(st - m_new)
        alpha = jnp.exp(m - m_new)
        l = alpha * l + jnp.sum(p, axis=0, keepdims=True)
        acc = alpha * acc + _dot(vt_ref[0, 0, kb], p.astype(BF16))
        return m_new, l, acc

    init = (jnp.full((1, bq), -jnp.inf, F32), jnp.zeros((1, bq), F32),
            jnp.zeros((MLA_V, bq), F32))
    _, l, acc = lax.fori_loop(0, nkb, body, init)
    ot_ref[0] = (acc / l).astype(BF16)


def _mla_attn(qn, qpe, kn, kpe, vt, batch, seq):
    nq = seq // MLA_BQ
    per_n = LANES // MLA_NOPE
    per_r = LANES // MLA_ROPE
    return pl.pallas_call(
        _mla_attn_kernel,
        grid=(batch, MLA_HEADS, nq),
        in_specs=[
            pl.BlockSpec((MLA_BQ, LANES), lambda b, h, i: (b * nq + i, h // per_n)),
            pl.BlockSpec((MLA_BQ, LANES), lambda b, h, i: (b * nq + i, h // per_r)),
            pl.BlockSpec((seq, LANES), lambda b, h, i: (b, h // per_n)),
            pl.BlockSpec((seq, LANES), lambda b, h, i: (b, 0)),
            pl.BlockSpec((1, 1, seq // MLA_BK, MLA_V, MLA_BK), lambda b, h, i: (b, h, 0, 0, 0)),
        ],
        out_specs=pl.BlockSpec((1, MLA_V, MLA_BQ), lambda b, h, i: (b, h, i)),
        out_shape=jax.ShapeDtypeStruct((batch, MLA_HEADS * MLA_V, seq), BF16),
        compiler_params=pltpu.CompilerParams(
            dimension_semantics=("arbitrary", "arbitrary", "arbitrary"),
            vmem_limit_bytes=VMEM_LIMIT),
        name="mla_attn",
    )(qn, qpe, kn, kpe, vt)


def _rot_cols(w):
    half = MLA_ROPE // 2
    return jnp.concatenate([-w[..., half:], w[..., :half]], axis=-1)


def _rope_tables(seq):
    inv = 1.0 / (ROPE_BASE ** (jnp.arange(0, MLA_ROPE, 2, dtype=F32) / MLA_ROPE))
    ang = jnp.arange(seq, dtype=F32)[:, None] * inv[None, :]
    cos = jnp.concatenate([jnp.cos(ang), jnp.cos(ang)], axis=-1)
    sin = jnp.concatenate([jnp.sin(ang), jnp.sin(ang)], axis=-1)
    return cos, sin


def kernel(x, p, norm_g, na_w_in, na_rpb, na_w_out, mla_w_in, mla_q_norm, mla_w_qb,
           mla_kv_norm, mla_w_kvb, mla_w_out, ple_norm, ple_w_gate, ple_w_proj, final_norm):
    batch, seq, d = x.shape
    m = batch * seq
    x2 = x.reshape(m, d)
    row = lambda v: v.reshape(1, -1).astype(F32)

    hd = NA_HEADS * NA_HEAD_DIM
    w_in0 = na_w_in[0]
    w_in0 = jnp.concatenate([w_in0[:, :hd] * (NA_HEAD_DIM ** -0.5), w_in0[:, hd:]], axis=1)
    qkvz = _na_inproj(x2, row(norm_g[0]), w_in0.astype(BF16))
    o0 = _na_attn(qkvz, _na_bias_table(na_rpb[0]), batch, seq)
    x2 = _post_block(x2, o0, qkvz, 3, na_w_out[0].astype(BF16), row(ple_norm[0]),
                     ple_w_gate[0].astype(BF16), p[0].reshape(m, PLE_DIM),
                     ple_w_proj[0].astype(BF16), row(final_norm),
                     feature_major=False, final_norm=False, seq=seq)

    w_in1 = mla_w_in[0]
    o1, o2 = MLA_Q_RANK, MLA_Q_RANK + MLA_KV_RANK
    o3 = o2 + MLA_ROPE
    w_kr = w_in1[:, o2:o3]
    rep = LANES // MLA_ROPE
    wc = jnp.concatenate([w_in1[:, :o2], jnp.tile(w_kr, (1, rep)),
                          jnp.tile(_rot_cols(w_kr), (1, rep))], axis=1).astype(BF16)
    wzt = w_in1[:, o3:].T.astype(BF16)

    scale = (MLA_NOPE + MLA_ROPE) ** -0.5
    wq3 = mla_w_qb[0].reshape(MLA_Q_RANK, MLA_HEADS, MLA_NOPE + MLA_ROPE) * scale
    wq_n = wq3[:, :, :MLA_NOPE].reshape(MLA_Q_RANK, -1)
    wq_p = wq3[:, :, MLA_NOPE:]
    wq = jnp.concatenate([wq_n, wq_p.reshape(MLA_Q_RANK, -1)], axis=1).astype(BF16)
    wqr = _rot_cols(wq_p).reshape(MLA_Q_RANK, -1).astype(BF16)

    wkv3 = mla_w_kvb[0].reshape(MLA_KV_RANK, MLA_HEADS, MLA_NOPE + MLA_V)
    wk = wkv3[:, :, :MLA_NOPE].reshape(MLA_KV_RANK, -1).astype(BF16)
    wvt = wkv3[:, :, MLA_NOPE:].reshape(MLA_KV_RANK, -1).T.astype(BF16)

    cos, sin = _rope_tables(seq)
    cosq, sinq = jnp.tile(cos, (1, MLA_HEADS)), jnp.tile(sin, (1, MLA_HEADS))
    cosk, sink = jnp.tile(cos, (1, rep)), jnp.tile(sin, (1, rep))

    qn, qpe, kn, kpe, vt, zt = _mla_inproj(
        x2, row(norm_g[1]), wc, wzt, row(mla_q_norm[0]), wq, wqr, row(mla_kv_norm[0]),
        wk, wvt, cosq, sinq, cosk, sink, batch, seq)
    ot = _mla_attn(qn, qpe, kn, kpe, vt, batch, seq)
    out = _post_block(x2, ot, zt, 0, mla_w_out[0].astype(BF16), row(ple_norm[1]),
                      ple_w_gate[1].astype(BF16), p[1].reshape(m, PLE_DIM),
                      ple_w_proj[1].astype(BF16), row(final_norm),
                      feature_major=True, final_norm=True, seq=seq)
    return out.reshape(batch, seq, d)
```

```python
import functools

import jax
import jax.numpy as jnp
from jax import lax
from jax.experimental import pallas as pl
from jax.experimental.pallas import tpu as pltpu

D_MODEL = 1024
GRID_W = 64
NA_WIN_ROWS = 8
NA_WIN_COLS = 16
NA_HEADS = 16
NA_HEAD_DIM = 64
MLA_HEADS = 16
MLA_Q_RANK = 384
MLA_KV_RANK = 256
MLA_NOPE = 64
MLA_ROPE = 32
MLA_V = 64
MLA_V_PAD = 80
ROPE_BASE = 10000.0
PLE_DIM = 256
EPS = 1e-6

LANES = 128
NEG_BIG = -1e30
VMEM_LIMIT = 56 * 1024 * 1024

TM_PROJ = 512
MLA_BQ = 512
MLA_BK = 512
LOG2E = 1.4426950408889634

BF16 = jnp.bfloat16
F32 = jnp.float32


def _rms(x, g):
    ms = jnp.mean(x * x, axis=-1, keepdims=True)
    return x * lax.rsqrt(ms + EPS) * g


def _sigmoid(x):
    return 1.0 / (1.0 + jnp.exp(-x))


def _dot(a, b):
    return jnp.dot(a, b, preferred_element_type=F32)


def _dot_nt(a, b):
    return lax.dot_general(a, b, (((1,), (1,)), ((), ())), preferred_element_type=F32)


def _dot_tn(a, b):
    return lax.dot_general(a, b, (((0,), (0,)), ((), ())), preferred_element_type=F32)


def _na_inproj_kernel(x_ref, g_ref, w_ref, o_ref, xn_ref):
    @pl.when(pl.program_id(1) == 0)
    def _():
        xn_ref[...] = _rms(x_ref[...], g_ref[...]).astype(BF16)

    o_ref[...] = _dot(xn_ref[...], w_ref[...]).astype(BF16)


def _na_inproj(x2, g, w):
    m = x2.shape[0]
    n = w.shape[1]
    tn = 1024
    return pl.pallas_call(
        _na_inproj_kernel,
        grid=(m // TM_PROJ, n // tn),
        in_specs=[
            pl.BlockSpec((TM_PROJ, D_MODEL), lambda i, j: (i, 0)),
            pl.BlockSpec((1, D_MODEL), lambda i, j: (0, 0)),
            pl.BlockSpec((D_MODEL, tn), lambda i, j: (0, j)),
        ],
        out_specs=pl.BlockSpec((TM_PROJ, tn), lambda i, j: (i, j)),
        out_shape=jax.ShapeDtypeStruct((m, n), BF16),
        scratch_shapes=[pltpu.VMEM((TM_PROJ, D_MODEL), BF16)],
        compiler_params=pltpu.CompilerParams(
            dimension_semantics=("arbitrary", "arbitrary"),
            vmem_limit_bytes=VMEM_LIMIT),
        name="na_inproj",
    )(x2, g, w)


NA_BAND = NA_WIN_ROWS * GRID_W
NA_SLABS = NA_HEADS * NA_HEAD_DIM // LANES
NA_ROW_OFFS = 2 * NA_WIN_ROWS - 1
NA_COL_OFFS = 2 * NA_WIN_COLS - 1
NA_PAIR_OFFS = NA_ROW_OFFS - 1


def _na_build_bias(rpb_ref, bias_scr):
    c = lax.broadcasted_iota(jnp.int32, (GRID_W, LANES), 0)
    lane = lax.broadcasted_iota(jnp.int32, (GRID_W, LANES), 1)
    j = lane % GRID_W
    cs = jnp.clip(c - NA_WIN_COLS // 2, 0, GRID_W - NA_WIN_COLS)
    valid = (j >= cs) & (j < cs + NA_WIN_COLS)
    diag = jnp.where(valid, j - c + (NA_WIN_COLS - 1), -1)
    second = lax.broadcasted_iota(jnp.int32, (1, LANES), 1) >= GRID_W

    def tile(idx, carry):
        h = idx // NA_PAIR_OFFS
        o = idx % NA_PAIR_OFFS
        base = (h * NA_ROW_OFFS + o) * NA_COL_OFFS
        acc = jnp.full((GRID_W, LANES), NEG_BIG, F32)
        for k in range(NA_COL_OFFS):
            val = jnp.where(second, rpb_ref[base + NA_COL_OFFS + k], rpb_ref[base + k]) * LOG2E
            acc = jnp.where(diag == k, val, acc)
        bias_scr[h, o] = acc
        return carry

    lax.fori_loop(0, NA_HEADS * NA_PAIR_OFFS, tile, 0)


def _na_attn_kernel(zero_ref, rpb_ref, q_ref, k_ref, v_ref, o_ref, bias_scr, s0_scr, s1_scr):
    r = pl.program_id(1)

    @pl.when((pl.program_id(0) == 0) & (r == 0))
    def _():
        _na_build_bias(rpb_ref, bias_scr)

    z = zero_ref[0]
    s_bufs = (s0_scr, s1_scr)
    rows = k_ref.shape[0] // GRID_W
    rs = jnp.clip(r - NA_WIN_ROWS // 2, 0, rows - NA_WIN_ROWS)
    start = pl.multiple_of(rs * GRID_W, GRID_W)
    off = (NA_WIN_ROWS - 1) - (r - rs)
    first = lax.broadcasted_iota(jnp.int32, (1, LANES), 1) < NA_HEAD_DIM

    def scores(slab, slot):
        cols = slice(slab * LANES, (slab + 1) * LANES)
        qs = q_ref[:, cols]
        zero = jnp.zeros_like(qs)
        q2 = jnp.concatenate([jnp.where(first, qs, zero), jnp.where(first, zero, qs)], axis=0)
        s_bufs[slot][z] = _dot_nt(q2, k_ref[pl.ds(start, NA_BAND), cols])

    def finish(slab, slot):
        cols = slice(slab * LANES, (slab + 1) * LANES)
        bias = jnp.concatenate(
            [jnp.concatenate([bias_scr[2 * slab + e, off + 2 * a] for a in range(NA_WIN_ROWS // 2)], axis=1)
             for e in range(2)], axis=0)
        s = s_bufs[slot][z] + bias
        m = jnp.max(s, axis=-1, keepdims=True)
        p = jnp.exp2(s - m)
        l = jnp.sum(p, axis=-1, keepdims=True)
        pv = _dot(p.astype(BF16), v_ref[pl.ds(start, NA_BAND), cols]) / l
        o_ref[:, cols] = jnp.where(first, pv[:GRID_W], pv[GRID_W:]).astype(BF16)

    scores(0, 0)
    for slab in range(NA_SLABS):
        if slab + 1 < NA_SLABS:
            scores(slab + 1, (slab + 1) % 2)
        finish(slab, slab % 2)


def _na_attn(qkvz, rpb_flat, batch, seq):
    hd = NA_HEADS * NA_HEAD_DIM
    rows = seq // GRID_W
    return pl.pallas_call(
        _na_attn_kernel,
        grid=(batch, rows),
        in_specs=[
            pl.BlockSpec(memory_space=pltpu.SMEM),
            pl.BlockSpec(memory_space=pltpu.SMEM),
            pl.BlockSpec((GRID_W, hd), lambda b, r: (b * rows + r, 0)),
            pl.BlockSpec((seq, hd), lambda b, r: (b, 1)),
            pl.BlockSpec((seq, hd), lambda b, r: (b, 2)),
        ],
        out_specs=pl.BlockSpec((GRID_W, hd), lambda b, r: (b * rows + r, 0)),
        out_shape=jax.ShapeDtypeStruct((batch * seq, hd), BF16),
        scratch_shapes=[pltpu.VMEM((NA_HEADS, NA_PAIR_OFFS, GRID_W, LANES), F32),
                        pltpu.VMEM((2, 2 * GRID_W, NA_BAND), F32),
                        pltpu.VMEM((2, 2 * GRID_W, NA_BAND), F32)],
        compiler_params=pltpu.CompilerParams(
            dimension_semantics=("arbitrary", "arbitrary"),
            vmem_limit_bytes=VMEM_LIMIT),
        name="na_attn",
    )(jnp.zeros((1,), jnp.int32), rpb_flat, qkvz, qkvz, qkvz)


def _post_kernel(x_ref, o_ref, z_ref, wo_ref, gp_ref, wg_ref, p_ref, wp_ref, gf_ref,
                 out_ref, *, feature_major, final_norm):
    if feature_major:
        z = z_ref[0].astype(F32)
        gated = (o_ref[0].astype(F32) * (z * _sigmoid(z))).astype(BF16)
        y = _dot_tn(gated, wo_ref[...])
    else:
        z = z_ref[...].astype(F32)
        gated = (o_ref[...].astype(F32) * (z * _sigmoid(z))).astype(BF16)
        y = _dot(gated, wo_ref[...])
    h = x_ref[...] + y
    hn = _rms(h, gp_ref[...]).astype(BF16)
    gate = _sigmoid(_dot(hn, wg_ref[...]))
    emb = _dot(p_ref[...].astype(BF16), wp_ref[...])
    xo = h + gate * emb
    if final_norm:
        xo = _rms(xo, gf_ref[...])
    out_ref[...] = xo


def _post_block(x2, o, z, z_col, w_out, g_ple, w_gate, p2, w_proj, g_final, *,
                feature_major, final_norm, seq):
    m = x2.shape[0]
    tm = TM_PROJ
    per_seq = seq // tm
    if feature_major:
        oz_block = (1, D_MODEL, tm)
        o_spec = pl.BlockSpec(oz_block, lambda i: (i // per_seq, 0, i % per_seq))
        z_spec = pl.BlockSpec(oz_block, lambda i: (i // per_seq, 0, i % per_seq))
    else:
        o_spec = pl.BlockSpec((tm, D_MODEL), lambda i: (i, 0))
        z_spec = pl.BlockSpec((tm, D_MODEL), lambda i: (i, z_col))
    full = lambda shape: pl.BlockSpec(shape, lambda i: (0,) * len(shape))
    kern = functools.partial(_post_kernel, feature_major=feature_major, final_norm=final_norm)
    return pl.pallas_call(
        kern,
        grid=(m // tm,),
        in_specs=[
            pl.BlockSpec((tm, D_MODEL), lambda i: (i, 0)),
            o_spec,
            z_spec,
            full((D_MODEL, D_MODEL)),
            full((1, D_MODEL)),
            full((D_MODEL, D_MODEL)),
            pl.BlockSpec((tm, PLE_DIM), lambda i: (i, 0)),
            full((PLE_DIM, D_MODEL)),
            full((1, D_MODEL)),
        ],
        out_specs=pl.BlockSpec((tm, D_MODEL), lambda i: (i, 0)),
        out_shape=jax.ShapeDtypeStruct((m, D_MODEL), F32),
        compiler_params=pltpu.CompilerParams(
            dimension_semantics=("arbitrary",),
            vmem_limit_bytes=VMEM_LIMIT),
        name="post_block_fm" if feature_major else "post_block",
    )(x2, o, z, w_out, g_ple, w_gate, p2, w_proj, g_final)


def _mla_inproj_kernel(x_ref, g_ref, wc_ref, wzt_ref, gq_ref, wq_ref, wqr_ref,
                       gkv_ref, wk_ref, wvt_ref, cq_ref, sq_ref, ck_ref, sk_ref,
                       qn_ref, qpe_ref, kn_ref, kpe_ref, vt_ref, zt_ref):
    xn = _rms(x_ref[...], g_ref[...]).astype(BF16)
    c = _dot(xn, wc_ref[...])
    zt_ref[0] = _dot_nt(wzt_ref[...], xn).astype(BF16)

    cq = _rms(c[:, :MLA_Q_RANK], gq_ref[...]).astype(BF16)
    ckv = _rms(c[:, MLA_Q_RANK:MLA_Q_RANK + MLA_KV_RANK], gkv_ref[...]).astype(BF16)
    off = MLA_Q_RANK + MLA_KV_RANK
    kpe = c[:, off:off + LANES] * ck_ref[...] + c[:, off + LANES:off + 2 * LANES] * sk_ref[...]
    kpe_ref[...] = kpe.astype(BF16)

    q = _dot(cq, wq_ref[...])
    qrot = _dot(cq, wqr_ref[...])
    nope = MLA_HEADS * MLA_NOPE
    qn_ref[...] = q[:, :nope].astype(BF16)
    qpe_ref[...] = (q[:, nope:] * cq_ref[...] + qrot * sq_ref[...]).astype(BF16)

    kn_ref[...] = _dot(ckv, wk_ref[...]).astype(BF16)
    vt = _dot_nt(wvt_ref[...], ckv).astype(BF16)
    tm = vt.shape[1]
    pad_rows = lax.broadcasted_iota(jnp.int32, (MLA_V_PAD - MLA_V, MLA_BK), 0)
    ones_row = jnp.where(pad_rows == 0, 1.0, 0.0).astype(BF16)
    for h in range(MLA_HEADS):
        for s in range(tm // MLA_BK):
            vt_ref[0, h, s, :MLA_V] = vt[h * MLA_V:(h + 1) * MLA_V, s * MLA_BK:(s + 1) * MLA_BK]
            vt_ref[0, h, s, MLA_V:] = ones_row


def _mla_inproj(x2, g, wc, wzt, gq, wq, wqr, gkv, wk, wvt, cosq, sinq, cosk, sink, batch, seq):
    m = x2.shape[0]
    tm = TM_PROJ
    per_seq = seq // tm
    sub = tm // MLA_BK
    full = lambda a: pl.BlockSpec(a.shape, lambda i: (0,) * a.ndim)
    rowblk = lambda n: pl.BlockSpec((tm, n), lambda i: (i, 0))
    tabblk = lambda n: pl.BlockSpec((tm, n), lambda i: (i % per_seq, 0))
    nope = MLA_HEADS * MLA_NOPE
    pe = MLA_HEADS * MLA_ROPE
    return pl.pallas_call(
        _mla_inproj_kernel,
        grid=(m // tm,),
        in_specs=[rowblk(D_MODEL), full(g), full(wc), full(wzt), full(gq), full(wq), full(wqr),
                  full(gkv), full(wk), full(wvt),
                  tabblk(pe), tabblk(pe), tabblk(LANES), tabblk(LANES)],
        out_specs=[
            rowblk(nope), rowblk(pe), rowblk(nope), rowblk(LANES),
            pl.BlockSpec((1, MLA_HEADS, sub, MLA_V_PAD, MLA_BK),
                         lambda i: (i // per_seq, 0, i % per_seq, 0, 0)),
            pl.BlockSpec((1, D_MODEL, tm), lambda i: (i // per_seq, 0, i % per_seq)),
        ],
        out_shape=[
            jax.ShapeDtypeStruct((m, nope), BF16),
            jax.ShapeDtypeStruct((m, pe), BF16),
            jax.ShapeDtypeStruct((m, nope), BF16),
            jax.ShapeDtypeStruct((m, LANES), BF16),
            jax.ShapeDtypeStruct((batch, MLA_HEADS, seq // MLA_BK, MLA_V_PAD, MLA_BK), BF16),
            jax.ShapeDtypeStruct((batch, D_MODEL, seq), BF16),
        ],
        compiler_params=pltpu.CompilerParams(
            dimension_semantics=("arbitrary",),
            vmem_limit_bytes=VMEM_LIMIT),
        name="mla_inproj",
    )(x2, g, wc, wzt, gq, wq, wqr, gkv, wk, wvt, cosq, sinq, cosk, sink)


def _mla_attn_kernel(zero_ref, qn_ref, qpe_ref, kn_ref, kpe_ref, vt_ref, ot_ref, q_scr, s0_scr, s1_scr):
    h = pl.program_id(1)
    z = zero_ref[0]
    s_bufs = (s0_scr, s1_scr)
    lane = lax.broadcasted_iota(jnp.int32, (1, LANES), 1)
    qn = qn_ref[...]
    qp = qpe_ref[...]
    q_scr[:, :LANES] = jnp.where(lane // MLA_NOPE == h % (LANES // MLA_NOPE), qn, jnp.zeros_like(qn))
    q_scr[:, LANES:] = jnp.where(lane // MLA_ROPE == h % (LANES // MLA_ROPE), qp, jnp.zeros_like(qp))
    bq = q_scr.shape[0]
    nkb = kn_ref.shape[0] // MLA_BK

    def scores(kb, slot):
        rows = slice(kb * MLA_BK, (kb + 1) * MLA_BK)
        k = jnp.concatenate([kn_ref[rows, :], kpe_ref[rows, :]], axis=1)
        s_bufs[slot][z] = _dot_nt(k, q_scr[...])

    def step(kb, slot, carry, prefetch):
        if prefetch:
            scores(kb + 1, 1 - slot)
        m, acc = carry
        m_new = jnp.maximum(m, jnp.max(s_bufs[slot][z], axis=0, keepdims=True))
        p = jnp.exp2(s_bufs[slot][z] - m_new)
        alpha = jnp.exp2(m - m_new)
        acc = alpha * acc + _dot(vt_ref[0, 0, kb], p.astype(BF16))
        return m_new, acc

    carry = (jnp.full((1, bq), -jnp.inf, F32), jnp.zeros((MLA_V_PAD, bq), F32))
    scores(0, 0)
    for kb in range(nkb):
        carry = step(kb, kb % 2, carry, kb + 1 < nkb)
    _, acc = carry
    ot_ref[0] = (acc[:MLA_V] / acc[MLA_V:MLA_V + 1]).astype(BF16)


def _mla_attn(qn, qpe, kn, kpe, vt, batch, seq):
    nq = seq // MLA_BQ
    per_n = LANES // MLA_NOPE
    per_r = LANES // MLA_ROPE
    return pl.pallas_call(
        _mla_attn_kernel,
        grid=(batch, MLA_HEADS, nq),
        in_specs=[
            pl.BlockSpec(memory_space=pltpu.SMEM),
            pl.BlockSpec((MLA_BQ, LANES), lambda b, h, i: (b * nq + i, h // per_n)),
            pl.BlockSpec((MLA_BQ, LANES), lambda b, h, i: (b * nq + i, h // per_r)),
            pl.BlockSpec((seq, LANES), lambda b, h, i: (b, h // per_n)),
            pl.BlockSpec((seq, LANES), lambda b, h, i: (b, 0)),
            pl.BlockSpec((1, 1, seq // MLA_BK, MLA_V_PAD, MLA_BK), lambda b, h, i: (b, h, 0, 0, 0)),
        ],
        out_specs=pl.BlockSpec((1, MLA_V, MLA_BQ), lambda b, h, i: (b, h, i)),
        out_shape=jax.ShapeDtypeStruct((batch, MLA_HEADS * MLA_V, seq), BF16),
        scratch_shapes=[pltpu.VMEM((MLA_BQ, 2 * LANES), BF16),
                        pltpu.VMEM((2, MLA_BK, MLA_BQ), F32),
                        pltpu.VMEM((2, MLA_BK, MLA_BQ), F32)],
        compiler_params=pltpu.CompilerParams(
            dimension_semantics=("arbitrary", "arbitrary", "arbitrary"),
            vmem_limit_bytes=VMEM_LIMIT),
        name="mla_attn",
    )(jnp.zeros((1,), jnp.int32), qn, qpe, kn, kpe, vt)


def _rot_cols(w):
    half = MLA_ROPE // 2
    return jnp.concatenate([-w[..., half:], w[..., :half]], axis=-1)


def _rope_tables(seq):
    inv = 1.0 / (ROPE_BASE ** (jnp.arange(0, MLA_ROPE, 2, dtype=F32) / MLA_ROPE))
    ang = jnp.arange(seq, dtype=F32)[:, None] * inv[None, :]
    cos = jnp.concatenate([jnp.cos(ang), jnp.cos(ang)], axis=-1)
    sin = jnp.concatenate([jnp.sin(ang), jnp.sin(ang)], axis=-1)
    return cos, sin


def kernel(x, p, norm_g, na_w_in, na_rpb, na_w_out, mla_w_in, mla_q_norm, mla_w_qb,
           mla_kv_norm, mla_w_kvb, mla_w_out, ple_norm, ple_w_gate, ple_w_proj, final_norm):
    batch, seq, d = x.shape
    m = batch * seq
    x2 = x.reshape(m, d)
    row = lambda v: v.reshape(1, -1).astype(F32)

    hd = NA_HEADS * NA_HEAD_DIM
    w_in0 = na_w_in[0]
    w_in0 = jnp.concatenate([w_in0[:, :hd] * (NA_HEAD_DIM ** -0.5 * LOG2E), w_in0[:, hd:]], axis=1)
    qkvz = _na_inproj(x2, row(norm_g[0]), w_in0.astype(BF16))
    o0 = _na_attn(qkvz, na_rpb[0].reshape(-1).astype(F32), batch, seq)
    x2 = _post_block(x2, o0, qkvz, 3, na_w_out[0].astype(BF16), row(ple_norm[0]),
                     ple_w_gate[0].astype(BF16), p[0].reshape(m, PLE_DIM),
                     ple_w_proj[0].astype(BF16), row(final_norm),
                     feature_major=False, final_norm=False, seq=seq)

    w_in1 = mla_w_in[0]
    o1, o2 = MLA_Q_RANK, MLA_Q_RANK + MLA_KV_RANK
    o3 = o2 + MLA_ROPE
    w_kr = w_in1[:, o2:o3]
    rep = LANES // MLA_ROPE
    wc = jnp.concatenate([w_in1[:, :o2], jnp.tile(w_kr, (1, rep)),
                          jnp.tile(_rot_cols(w_kr), (1, rep))], axis=1).astype(BF16)
    wzt = w_in1[:, o3:].T.astype(BF16)

    scale = (MLA_NOPE + MLA_ROPE) ** -0.5 * LOG2E
    wq3 = mla_w_qb[0].reshape(MLA_Q_RANK, MLA_HEADS, MLA_NOPE + MLA_ROPE) * scale
    wq_n = wq3[:, :, :MLA_NOPE].reshape(MLA_Q_RANK, -1)
    wq_p = wq3[:, :, MLA_NOPE:]
    wq = jnp.concatenate([wq_n, wq_p.reshape(MLA_Q_RANK, -1)], axis=1).astype(BF16)
    wqr = _rot_cols(wq_p).reshape(MLA_Q_RANK, -1).astype(BF16)

    wkv3 = mla_w_kvb[0].reshape(MLA_KV_RANK, MLA_HEADS, MLA_NOPE + MLA_V)
    wk = wkv3[:, :, :MLA_NOPE].reshape(MLA_KV_RANK, -1).astype(BF16)
    wvt = wkv3[:, :, MLA_NOPE:].reshape(MLA_KV_RANK, -1).T.astype(BF16)

    cos, sin = _rope_tables(seq)
    cosq, sinq = jnp.tile(cos, (1, MLA_HEADS)), jnp.tile(sin, (1, MLA_HEADS))
    cosk, sink = jnp.tile(cos, (1, rep)), jnp.tile(sin, (1, rep))

    qn, qpe, kn, kpe, vt, zt = _mla_inproj(
        x2, row(norm_g[1]), wc, wzt, row(mla_q_norm[0]), wq, wqr, row(mla_kv_norm[0]),
        wk, wvt, cosq, sinq, cosk, sink, batch, seq)
    ot = _mla_attn(qn, qpe, kn, kpe, vt, batch, seq)
    out = _post_block(x2, ot, zt, 0, mla_w_out[0].astype(BF16), row(ple_norm[1]),
                      ple_w_gate[1].astype(BF16), p[1].reshape(m, PLE_DIM),
                      ple_w_proj[1].astype(BF16), row(final_norm),
                      feature_major=True, final_norm=True, seq=seq)
    return out.reshape(batch, seq, d)
```

```python
import functools

import jax
import jax.numpy as jnp
from jax import lax
from jax.experimental import pallas as pl
from jax.experimental.pallas import tpu as pltpu

D_MODEL = 1024
GRID_W = 64
NA_WIN_ROWS = 8
NA_WIN_COLS = 16
NA_HEADS = 16
NA_HEAD_DIM = 64
MLA_HEADS = 16
MLA_Q_RANK = 384
MLA_KV_RANK = 256
MLA_NOPE = 64
MLA_ROPE = 32
MLA_V = 64
MLA_V_PAD = 80
ROPE_BASE = 10000.0
PLE_DIM = 256
EPS = 1e-6

LANES = 128
NEG_BIG = -1e30
VMEM_LIMIT = 56 * 1024 * 1024

TM_PROJ = 512
MLA_BQ = 512
MLA_QSTEPS = 2
MLA_BK = 512
MLA_DENOM_LIMIT = 3e38
LOG2E = 1.4426950408889634

BF16 = jnp.bfloat16
F32 = jnp.float32


def _rms(x, g):
    ms = jnp.mean(x * x, axis=-1, keepdims=True)
    return x * lax.rsqrt(ms + EPS) * g


def _sigmoid(x):
    return 1.0 / (1.0 + jnp.exp(-x))


def _dot(a, b):
    return jnp.dot(a, b, preferred_element_type=F32)


def _dot_nt(a, b):
    return lax.dot_general(a, b, (((1,), (1,)), ((), ())), preferred_element_type=F32)


def _dot_tn(a, b):
    return lax.dot_general(a, b, (((0,), (0,)), ((), ())), preferred_element_type=F32)


def _na_inproj_kernel(x_ref, g_ref, w_ref, o_ref, xn_ref):
    @pl.when(pl.program_id(1) == 0)
    def _():
        xn_ref[...] = _rms(x_ref[...], g_ref[...]).astype(BF16)

    o_ref[...] = _dot(xn_ref[...], w_ref[...]).astype(BF16)


def _na_inproj(x2, g, w):
    m = x2.shape[0]
    n = w.shape[1]
    tn = 1024
    return pl.pallas_call(
        _na_inproj_kernel,
        grid=(m // TM_PROJ, n // tn),
        in_specs=[
            pl.BlockSpec((TM_PROJ, D_MODEL), lambda i, j: (i, 0)),
            pl.BlockSpec((1, D_MODEL), lambda i, j: (0, 0)),
            pl.BlockSpec((D_MODEL, tn), lambda i, j: (0, j)),
        ],
        out_specs=pl.BlockSpec((TM_PROJ, tn), lambda i, j: (i, j)),
        out_shape=jax.ShapeDtypeStruct((m, n), BF16),
        scratch_shapes=[pltpu.VMEM((TM_PROJ, D_MODEL), BF16)],
        compiler_params=pltpu.CompilerParams(
            dimension_semantics=("arbitrary", "arbitrary"),
            vmem_limit_bytes=VMEM_LIMIT),
        name="na_inproj",
    )(x2, g, w)


NA_BAND = NA_WIN_ROWS * GRID_W
NA_SLABS = NA_HEADS * NA_HEAD_DIM // LANES
NA_ROW_OFFS = 2 * NA_WIN_ROWS - 1
NA_COL_OFFS = 2 * NA_WIN_COLS - 1
NA_PAIR_OFFS = NA_ROW_OFFS - 1


def _na_build_bias(rpb_ref, bias_scr):
    c = lax.broadcasted_iota(jnp.int32, (GRID_W, LANES), 0)
    lane = lax.broadcasted_iota(jnp.int32, (GRID_W, LANES), 1)
    j = lane % GRID_W
    cs = jnp.clip(c - NA_WIN_COLS // 2, 0, GRID_W - NA_WIN_COLS)
    valid = (j >= cs) & (j < cs + NA_WIN_COLS)
    diag = jnp.where(valid, j - c + (NA_WIN_COLS - 1), -1)
    second = lax.broadcasted_iota(jnp.int32, (1, LANES), 1) >= GRID_W

    def tile(idx, carry):
        h = idx // NA_PAIR_OFFS
        o = idx % NA_PAIR_OFFS
        base = (h * NA_ROW_OFFS + o) * NA_COL_OFFS
        acc = jnp.full((GRID_W, LANES), NEG_BIG, F32)
        for k in range(NA_COL_OFFS):
            val = jnp.where(second, rpb_ref[base + NA_COL_OFFS + k], rpb_ref[base + k]) * LOG2E
            acc = jnp.where(diag == k, val, acc)
        bias_scr[h, o] = acc
        return carry

    lax.fori_loop(0, NA_HEADS * NA_PAIR_OFFS, tile, 0)


def _na_attn_kernel(zero_ref, rpb_ref, q_ref, k_ref, v_ref, o_ref, bias_scr, s0_scr, s1_scr):
    r = pl.program_id(1)

    @pl.when((pl.program_id(0) == 0) & (r == 0))
    def _():
        _na_build_bias(rpb_ref, bias_scr)

    z = zero_ref[0]
    s_bufs = (s0_scr, s1_scr)
    rows = k_ref.shape[0] // GRID_W
    rs = jnp.clip(r - NA_WIN_ROWS // 2, 0, rows - NA_WIN_ROWS)
    start = pl.multiple_of(rs * GRID_W, GRID_W)
    off = (NA_WIN_ROWS - 1) - (r - rs)
    first = lax.broadcasted_iota(jnp.int32, (1, LANES), 1) < NA_HEAD_DIM

    def scores(slab, slot):
        cols = slice(slab * LANES, (slab + 1) * LANES)
        qs = q_ref[:, cols]
        zero = jnp.zeros_like(qs)
        q2 = jnp.concatenate([jnp.where(first, qs, zero), jnp.where(first, zero, qs)], axis=0)
        s_bufs[slot][z] = _dot_nt(q2, k_ref[pl.ds(start, NA_BAND), cols])

    def finish(slab, slot):
        cols = slice(slab * LANES, (slab + 1) * LANES)
        bias = jnp.concatenate(
            [jnp.concatenate([bias_scr[2 * slab + e, off + 2 * a] for a in range(NA_WIN_ROWS // 2)], axis=1)
             for e in range(2)], axis=0)
        s = s_bufs[slot][z] + bias
        m = jnp.max(s, axis=-1, keepdims=True)
        p = jnp.exp2(s - m)
        l = jnp.sum(p, axis=-1, keepdims=True)
        pv = _dot(p.astype(BF16), v_ref[pl.ds(start, NA_BAND), cols]) / l
        o_ref[:, cols] = jnp.where(first, pv[:GRID_W], pv[GRID_W:]).astype(BF16)

    scores(0, 0)
    for slab in range(NA_SLABS):
        if slab + 1 < NA_SLABS:
            scores(slab + 1, (slab + 1) % 2)
        finish(slab, slab % 2)


def _na_attn(qkvz, rpb_flat, batch, seq):
    hd = NA_HEADS * NA_HEAD_DIM
    rows = seq // GRID_W
    return pl.pallas_call(
        _na_attn_kernel,
        grid=(batch, rows),
        in_specs=[
            pl.BlockSpec(memory_space=pltpu.SMEM),
            pl.BlockSpec(memory_space=pltpu.SMEM),
            pl.BlockSpec((GRID_W, hd), lambda b, r: (b * rows + r, 0)),
            pl.BlockSpec((seq, hd), lambda b, r: (b, 1)),
            pl.BlockSpec((seq, hd), lambda b, r: (b, 2)),
        ],
        out_specs=pl.BlockSpec((GRID_W, hd), lambda b, r: (b * rows + r, 0)),
        out_shape=jax.ShapeDtypeStruct((batch * seq, hd), BF16),
        scratch_shapes=[pltpu.VMEM((NA_HEADS, NA_PAIR_OFFS, GRID_W, LANES), F32),
                        pltpu.VMEM((2, 2 * GRID_W, NA_BAND), F32),
                        pltpu.VMEM((2, 2 * GRID_W, NA_BAND), F32)],
        compiler_params=pltpu.CompilerParams(
            dimension_semantics=("arbitrary", "arbitrary"),
            vmem_limit_bytes=VMEM_LIMIT),
        name="na_attn",
    )(jnp.zeros((1,), jnp.int32), rpb_flat, qkvz, qkvz, qkvz)


def _post_kernel(x_ref, o_ref, z_ref, wo_ref, gp_ref, wg_ref, p_ref, wp_ref, gf_ref,
                 out_ref, *, feature_major, final_norm):
    if feature_major:
        z = z_ref[0].astype(F32)
        gated = (o_ref[0].astype(F32) * (z * _sigmoid(z))).astype(BF16)
        y = _dot_tn(gated, wo_ref[...])
    else:
        z = z_ref[...].astype(F32)
        gated = (o_ref[...].astype(F32) * (z * _sigmoid(z))).astype(BF16)
        y = _dot(gated, wo_ref[...])
    h = x_ref[...] + y
    hn = _rms(h, gp_ref[...]).astype(BF16)
    gate = _sigmoid(_dot(hn, wg_ref[...]))
    emb = _dot(p_ref[...].astype(BF16), wp_ref[...])
    xo = h + gate * emb
    if final_norm:
        xo = _rms(xo, gf_ref[...])
    out_ref[...] = xo


def _post_block(x2, o, z, z_col, w_out, g_ple, w_gate, p2, w_proj, g_final, *,
                feature_major, final_norm, seq):
    m = x2.shape[0]
    tm = TM_PROJ
    per_seq = seq // tm
    if feature_major:
        oz_block = (1, D_MODEL, tm)
        o_spec = pl.BlockSpec(oz_block, lambda i: (i // per_seq, 0, i % per_seq))
        z_spec = pl.BlockSpec(oz_block, lambda i: (i // per_seq, 0, i % per_seq))
    else:
        o_spec = pl.BlockSpec((tm, D_MODEL), lambda i: (i, 0))
        z_spec = pl.BlockSpec((tm, D_MODEL), lambda i: (i, z_col))
    full = lambda shape: pl.BlockSpec(shape, lambda i: (0,) * len(shape))
    kern = functools.partial(_post_kernel, feature_major=feature_major, final_norm=final_norm)
    return pl.pallas_call(
        kern,
        grid=(m // tm,),
        in_specs=[
            pl.BlockSpec((tm, D_MODEL), lambda i: (i, 0)),
            o_spec,
            z_spec,
            full((D_MODEL, D_MODEL)),
            full((1, D_MODEL)),
            full((D_MODEL, D_MODEL)),
            pl.BlockSpec((tm, PLE_DIM), lambda i: (i, 0)),
            full((PLE_DIM, D_MODEL)),
            full((1, D_MODEL)),
        ],
        out_specs=pl.BlockSpec((tm, D_MODEL), lambda i: (i, 0)),
        out_shape=jax.ShapeDtypeStruct((m, D_MODEL), F32),
        compiler_params=pltpu.CompilerParams(
            dimension_semantics=("arbitrary",),
            vmem_limit_bytes=VMEM_LIMIT),
        name="post_block_fm" if feature_major else "post_block",
    )(x2, o, z, w_out, g_ple, w_gate, p2, w_proj, g_final)


def _mla_inproj_kernel(x_ref, g_ref, wc_ref, wzt_ref, gq_ref, wq_ref, wqr_ref,
                       gkv_ref, wk_ref, wvt_ref, cq_ref, sq_ref, ck_ref, sk_ref,
                       qn_ref, qpe_ref, kn_ref, kpe_ref, vt_ref, zt_ref):
    xn = _rms(x_ref[...], g_ref[...]).astype(BF16)
    c = _dot(xn, wc_ref[...])
    zt_ref[0] = _dot_nt(wzt_ref[...], xn).astype(BF16)

    cq = _rms(c[:, :MLA_Q_RANK], gq_ref[...]).astype(BF16)
    ckv = _rms(c[:, MLA_Q_RANK:MLA_Q_RANK + MLA_KV_RANK], gkv_ref[...]).astype(BF16)
    off = MLA_Q_RANK + MLA_KV_RANK
    kpe = c[:, off:off + LANES] * ck_ref[...] + c[:, off + LANES:off + 2 * LANES] * sk_ref[...]
    kpe_ref[...] = kpe.astype(BF16)

    qt = _dot_nt(wq_ref[...], cq)
    qrot = _dot_nt(wqr_ref[...], cq)
    nope = MLA_HEADS * MLA_NOPE
    qn_ref[0] = qt[:nope].astype(BF16)
    qpe_ref[0] = (qt[nope:] * cq_ref[...] + qrot * sq_ref[...]).astype(BF16)

    kn_ref[...] = _dot(ckv, wk_ref[...]).astype(BF16)
    vt = _dot_nt(wvt_ref[...], ckv).astype(BF16)
    tm = vt.shape[1]
    pad_rows = lax.broadcasted_iota(jnp.int32, (MLA_V_PAD - MLA_V, MLA_BK), 0)
    ones_row = jnp.where(pad_rows == 0, 1.0, 0.0).astype(BF16)
    for h in range(MLA_HEADS):
        for s in range(tm // MLA_BK):
            vt_ref[0, h, s, :MLA_V] = vt[h * MLA_V:(h + 1) * MLA_V, s * MLA_BK:(s + 1) * MLA_BK]
            vt_ref[0, h, s, MLA_V:] = ones_row


def _mla_inproj(x2, g, wc, wzt, gq, wq, wqr, gkv, wk, wvt, cosq, sinq, cosk, sink, batch, seq):
    m = x2.shape[0]
    tm = TM_PROJ
    per_seq = seq // tm
    sub = tm // MLA_BK
    full = lambda a: pl.BlockSpec(a.shape, lambda i: (0,) * a.ndim)
    rowblk = lambda n: pl.BlockSpec((tm, n), lambda i: (i, 0))
    tabblk = lambda n: pl.BlockSpec((tm, n), lambda i: (i % per_seq, 0))
    fmtab = lambda n: pl.BlockSpec((n, tm), lambda i: (0, i % per_seq))
    fmblk = lambda n: pl.BlockSpec((1, n, tm), lambda i: (i // per_seq, 0, i % per_seq))
    nope = MLA_HEADS * MLA_NOPE
    pe = MLA_HEADS * MLA_ROPE
    return pl.pallas_call(
        _mla_inproj_kernel,
        grid=(m // tm,),
        in_specs=[rowblk(D_MODEL), full(g), full(wc), full(wzt), full(gq), full(wq), full(wqr),
                  full(gkv), full(wk), full(wvt),
                  fmtab(pe), fmtab(pe), tabblk(LANES), tabblk(LANES)],
        out_specs=[
            fmblk(nope), fmblk(pe), rowblk(nope), rowblk(LANES),
            pl.BlockSpec((1, MLA_HEADS, sub, MLA_V_PAD, MLA_BK),
                         lambda i: (i // per_seq, 0, i % per_seq, 0, 0)),
            pl.BlockSpec((1, D_MODEL, tm), lambda i: (i // per_seq, 0, i % per_seq)),
        ],
        out_shape=[
            jax.ShapeDtypeStruct((batch, nope, seq), BF16),
            jax.ShapeDtypeStruct((batch, pe, seq), BF16),
            jax.ShapeDtypeStruct((m, nope), BF16),
            jax.ShapeDtypeStruct((m, LANES), BF16),
            jax.ShapeDtypeStruct((batch, MLA_HEADS, seq // MLA_BK, MLA_V_PAD, MLA_BK), BF16),
            jax.ShapeDtypeStruct((batch, D_MODEL, seq), BF16),
        ],
        compiler_params=pltpu.CompilerParams(
            dimension_semantics=("arbitrary",),
            vmem_limit_bytes=VMEM_LIMIT),
        name="mla_inproj",
    )(x2, g, wc, wzt, gq, wq, wqr, gkv, wk, wvt, cosq, sinq, cosk, sink)


def _mla_attn_rescaling(q_scr, kn_ref, kpe_ref, vt_ref, ot_ref):
    nkb = kn_ref.shape[0] // MLA_BK
    for qb in range(MLA_QSTEPS):
        cols = slice(qb * MLA_BQ, (qb + 1) * MLA_BQ)

        def body(kb, carry, cols=cols):
            m, acc = carry
            off = pl.multiple_of(kb * MLA_BK, MLA_BK)
            k = jnp.concatenate([kn_ref[pl.ds(off, MLA_BK), :], kpe_ref[pl.ds(off, MLA_BK), :]], axis=1)
            st = _dot(k, q_scr[:, cols])
            m_new = jnp.maximum(m, jnp.max(st, axis=0, keepdims=True))
            p = jnp.exp2(st - m_new)
            acc = jnp.exp2(m - m_new) * acc + _dot(vt_ref[0, 0, kb], p.astype(BF16))
            return m_new, acc

        init = (jnp.full((1, MLA_BQ), -jnp.inf, F32), jnp.zeros((MLA_V_PAD, MLA_BQ), F32))
        _, acc = lax.fori_loop(0, nkb, body, init)
        ot_ref[0, :, cols] = (acc[:MLA_V] / acc[MLA_V:MLA_V + 1]).astype(BF16)


def _mla_attn_kernel(zero_ref, qn_ref, qpe_ref, kn_ref, kpe_ref, vt_ref, ot_ref, q_scr, s0_scr, s1_scr):
    h = pl.program_id(1)
    row = lax.broadcasted_iota(jnp.int32, (LANES, 1), 0)
    qn = qn_ref[0]
    qp = qpe_ref[0]
    q_scr[:LANES] = jnp.where(row // MLA_NOPE == h % (LANES // MLA_NOPE), qn, jnp.zeros_like(qn))
    q_scr[LANES:] = jnp.where(row // MLA_ROPE == h % (LANES // MLA_ROPE), qp, jnp.zeros_like(qp))
    nkb = kn_ref.shape[0] // MLA_BK

    z = zero_ref[0]
    s_bufs = (s0_scr, s1_scr)
    blocks = [(qb, kb) for qb in range(MLA_QSTEPS) for kb in range(nkb)]

    def scores(i):
        qb, kb = blocks[i]
        rows = slice(kb * MLA_BK, (kb + 1) * MLA_BK)
        k = jnp.concatenate([kn_ref[rows, :], kpe_ref[rows, :]], axis=1)
        s_bufs[i % 2][z] = _dot(k, q_scr[:, qb * MLA_BQ:(qb + 1) * MLA_BQ])

    finite = None
    scores(0)
    for i, (qb, kb) in enumerate(blocks):
        if i + 1 < len(blocks):
            scores(i + 1)
        if kb == 0:
            m_ref = jnp.max(s_bufs[i % 2][z], axis=0, keepdims=True)
            acc = jnp.zeros((MLA_V_PAD, MLA_BQ), F32)
        acc = acc + _dot(vt_ref[0, 0, kb], jnp.exp2(s_bufs[i % 2][z] - m_ref).astype(BF16))
        if kb == nkb - 1:
            denom = acc[MLA_V:MLA_V + 1]
            ot_ref[0, :, qb * MLA_BQ:(qb + 1) * MLA_BQ] = (acc[:MLA_V] / denom).astype(BF16)
            ok = jnp.max(denom) < MLA_DENOM_LIMIT
            finite = ok if finite is None else jnp.logical_and(finite, ok)

    @pl.when(jnp.logical_not(finite))
    def _():
        _mla_attn_rescaling(q_scr, kn_ref, kpe_ref, vt_ref, ot_ref)


def _mla_attn(qn, qpe, kn, kpe, vt, batch, seq):
    bq = MLA_QSTEPS * MLA_BQ
    per_n = LANES // MLA_NOPE
    per_r = LANES // MLA_ROPE
    return pl.pallas_call(
        _mla_attn_kernel,
        grid=(batch, MLA_HEADS, seq // bq),
        in_specs=[
            pl.BlockSpec(memory_space=pltpu.SMEM),
            pl.BlockSpec((1, LANES, bq), lambda b, h, i: (b, h // per_n, i)),
            pl.BlockSpec((1, LANES, bq), lambda b, h, i: (b, h // per_r, i)),
            pl.BlockSpec((seq, LANES), lambda b, h, i: (b, h // per_n)),
            pl.BlockSpec((seq, LANES), lambda b, h, i: (b, 0)),
            pl.BlockSpec((1, 1, seq // MLA_BK, MLA_V_PAD, MLA_BK), lambda b, h, i: (b, h, 0, 0, 0)),
        ],
        out_specs=pl.BlockSpec((1, MLA_V, bq), lambda b, h, i: (b, h, i)),
        out_shape=jax.ShapeDtypeStruct((batch, MLA_HEADS * MLA_V, seq), BF16),
        scratch_shapes=[pltpu.VMEM((2 * LANES, bq), BF16),
                        pltpu.VMEM((2, MLA_BK, MLA_BQ), F32),
                        pltpu.VMEM((2, MLA_BK, MLA_BQ), F32)],
        compiler_params=pltpu.CompilerParams(
            dimension_semantics=("arbitrary", "arbitrary", "arbitrary"),
            vmem_limit_bytes=VMEM_LIMIT),
        name="mla_attn",
    )(jnp.zeros((1,), jnp.int32), qn, qpe, kn, kpe, vt)


def _rot_cols(w):
    half = MLA_ROPE // 2
    return jnp.concatenate([-w[..., half:], w[..., :half]], axis=-1)


def _rope_tables(seq):
    inv = 1.0 / (ROPE_BASE ** (jnp.arange(0, MLA_ROPE, 2, dtype=F32) / MLA_ROPE))
    ang = jnp.arange(seq, dtype=F32)[:, None] * inv[None, :]
    cos = jnp.concatenate([jnp.cos(ang), jnp.cos(ang)], axis=-1)
    sin = jnp.concatenate([jnp.sin(ang), jnp.sin(ang)], axis=-1)
    return cos, sin


def kernel(x, p, norm_g, na_w_in, na_rpb, na_w_out, mla_w_in, mla_q_norm, mla_w_qb,
           mla_kv_norm, mla_w_kvb, mla_w_out, ple_norm, ple_w_gate, ple_w_proj, final_norm):
    batch, seq, d = x.shape
    m = batch * seq
    x2 = x.reshape(m, d)
    row = lambda v: v.reshape(1, -1).astype(F32)

    hd = NA_HEADS * NA_HEAD_DIM
    w_in0 = na_w_in[0]
    w_in0 = jnp.concatenate([w_in0[:, :hd] * (NA_HEAD_DIM ** -0.5 * LOG2E), w_in0[:, hd:]], axis=1)
    qkvz = _na_inproj(x2, row(norm_g[0]), w_in0.astype(BF16))
    o0 = _na_attn(qkvz, na_rpb[0].reshape(-1).astype(F32), batch, seq)
    x2 = _post_block(x2, o0, qkvz, 3, na_w_out[0].astype(BF16), row(ple_norm[0]),
                     ple_w_gate[0].astype(BF16), p[0].reshape(m, PLE_DIM),
                     ple_w_proj[0].astype(BF16), row(final_norm),
                     feature_major=False, final_norm=False, seq=seq)

    w_in1 = mla_w_in[0]
    o1, o2 = MLA_Q_RANK, MLA_Q_RANK + MLA_KV_RANK
    o3 = o2 + MLA_ROPE
    w_kr = w_in1[:, o2:o3]
    rep = LANES // MLA_ROPE
    wc = jnp.concatenate([w_in1[:, :o2], jnp.tile(w_kr, (1, rep)),
                          jnp.tile(_rot_cols(w_kr), (1, rep))], axis=1).astype(BF16)
    wzt = w_in1[:, o3:].T.astype(BF16)

    scale = (MLA_NOPE + MLA_ROPE) ** -0.5 * LOG2E
    wq3 = mla_w_qb[0].reshape(MLA_Q_RANK, MLA_HEADS, MLA_NOPE + MLA_ROPE) * scale
    wq_n = wq3[:, :, :MLA_NOPE].reshape(MLA_Q_RANK, -1)
    wq_p = wq3[:, :, MLA_NOPE:]
    wq = jnp.concatenate([wq_n, wq_p.reshape(MLA_Q_RANK, -1)], axis=1).T.astype(BF16)
    wqr = _rot_cols(wq_p).reshape(MLA_Q_RANK, -1).T.astype(BF16)

    wkv3 = mla_w_kvb[0].reshape(MLA_KV_RANK, MLA_HEADS, MLA_NOPE + MLA_V)
    wk = wkv3[:, :, :MLA_NOPE].reshape(MLA_KV_RANK, -1).astype(BF16)
    wvt = wkv3[:, :, MLA_NOPE:].reshape(MLA_KV_RANK, -1).T.astype(BF16)

    cos, sin = _rope_tables(seq)
    cosq, sinq = jnp.tile(cos, (1, MLA_HEADS)).T, jnp.tile(sin, (1, MLA_HEADS)).T
    cosk, sink = jnp.tile(cos, (1, rep)), jnp.tile(sin, (1, rep))

    qn, qpe, kn, kpe, vt, zt = _mla_inproj(
        x2, row(norm_g[1]), wc, wzt, row(mla_q_norm[0]), wq, wqr, row(mla_kv_norm[0]),
        wk, wvt, cosq, sinq, cosk, sink, batch, seq)
    ot = _mla_attn(qn, qpe, kn, kpe, vt, batch, seq)
    out = _post_block(x2, ot, zt, 0, mla_w_out[0].astype(BF16), row(ple_norm[1]),
                      ple_w_gate[1].astype(BF16), p[1].reshape(m, PLE_DIM),
                      ple_w_proj[1].astype(BF16), row(final_norm),
                      feature_major=True, final_norm=True, seq=seq)
    return out.reshape(batch, seq, d)
```

```python
import functools

import jax
import jax.numpy as jnp
from jax import lax
from jax.experimental import pallas as pl
from jax.experimental.pallas import tpu as pltpu

D_MODEL = 1024
GRID_W = 64
NA_WIN_ROWS = 8
NA_WIN_COLS = 16
NA_HEADS = 16
NA_HEAD_DIM = 64
MLA_HEADS = 16
MLA_Q_RANK = 384
MLA_KV_RANK = 256
MLA_NOPE = 64
MLA_ROPE = 32
MLA_V = 64
MLA_V_PAD = 80
ROPE_BASE = 10000.0
PLE_DIM = 256
EPS = 1e-6

LANES = 128
NEG_BIG = -1e30
VMEM_LIMIT = 56 * 1024 * 1024

TM_PROJ = 512
MLA_BQ = 512
MLA_QSTEPS = 2
MLA_BK = 512
MLA_DENOM_LIMIT = 3e38
LOG2E = 1.4426950408889634

BF16 = jnp.bfloat16
F32 = jnp.float32


def _rms(x, g):
    ms = jnp.mean(x * x, axis=-1, keepdims=True)
    return x * lax.rsqrt(ms + EPS) * g


def _sigmoid(x):
    return 1.0 / (1.0 + jnp.exp(-x))


def _dot(a, b):
    return jnp.dot(a, b, preferred_element_type=F32)


def _dot_nt(a, b):
    return lax.dot_general(a, b, (((1,), (1,)), ((), ())), preferred_element_type=F32)


def _dot_tn(a, b):
    return lax.dot_general(a, b, (((0,), (0,)), ((), ())), preferred_element_type=F32)


def _na_inproj_kernel(x_ref, g_ref, w_ref, o_ref, xn_ref):
    @pl.when(pl.program_id(1) == 0)
    def _():
        xn_ref[...] = _rms(x_ref[...], g_ref[...]).astype(BF16)

    o_ref[...] = _dot(xn_ref[...], w_ref[...]).astype(BF16)


def _na_inproj(x2, g, w):
    m = x2.shape[0]
    n = w.shape[1]
    tn = n
    return pl.pallas_call(
        _na_inproj_kernel,
        grid=(m // TM_PROJ, n // tn),
        in_specs=[
            pl.BlockSpec((TM_PROJ, D_MODEL), lambda i, j: (i, 0)),
            pl.BlockSpec((1, D_MODEL), lambda i, j: (0, 0)),
            pl.BlockSpec((D_MODEL, tn), lambda i, j: (0, j)),
        ],
        out_specs=pl.BlockSpec((TM_PROJ, tn), lambda i, j: (i, j)),
        out_shape=jax.ShapeDtypeStruct((m, n), BF16),
        scratch_shapes=[pltpu.VMEM((TM_PROJ, D_MODEL), BF16)],
        compiler_params=pltpu.CompilerParams(
            dimension_semantics=("arbitrary", "arbitrary"),
            vmem_limit_bytes=VMEM_LIMIT),
        name="na_inproj",
    )(x2, g, w)


NA_BAND = NA_WIN_ROWS * GRID_W
NA_SLABS = NA_HEADS * NA_HEAD_DIM // LANES
NA_ROW_OFFS = 2 * NA_WIN_ROWS - 1
NA_COL_OFFS = 2 * NA_WIN_COLS - 1
NA_PAIR_OFFS = NA_ROW_OFFS - 1
NA_ROWS_PER_STEP = 4


def _na_build_bias(rpb_ref, bias_scr):
    c = lax.broadcasted_iota(jnp.int32, (GRID_W, LANES), 0)
    lane = lax.broadcasted_iota(jnp.int32, (GRID_W, LANES), 1)
    j = lane % GRID_W
    cs = jnp.clip(c - NA_WIN_COLS // 2, 0, GRID_W - NA_WIN_COLS)
    valid = (j >= cs) & (j < cs + NA_WIN_COLS)
    diag = jnp.where(valid, j - c + (NA_WIN_COLS - 1), -1)
    second = lax.broadcasted_iota(jnp.int32, (1, LANES), 1) >= GRID_W

    def tile(idx, carry):
        h = idx // NA_PAIR_OFFS
        o = idx % NA_PAIR_OFFS
        base = (h * NA_ROW_OFFS + o) * NA_COL_OFFS
        acc = jnp.full((GRID_W, LANES), NEG_BIG, F32)
        for k in range(NA_COL_OFFS):
            val = jnp.where(second, rpb_ref[base + NA_COL_OFFS + k], rpb_ref[base + k]) * LOG2E
            acc = jnp.where(diag == k, val, acc)
        bias_scr[h, o] = acc
        return carry

    lax.fori_loop(0, NA_HEADS * NA_PAIR_OFFS, tile, 0)


def _na_attn_kernel(zero_ref, rpb_ref, q_ref, k_ref, v_ref, o_ref, bias_scr, s0_scr, s1_scr):
    first_row = pl.program_id(1) * NA_ROWS_PER_STEP

    @pl.when((pl.program_id(0) == 0) & (first_row == 0))
    def _():
        _na_build_bias(rpb_ref, bias_scr)

    z = zero_ref[0]
    s_bufs = (s0_scr, s1_scr)
    rows = k_ref.shape[0] // GRID_W
    first = lax.broadcasted_iota(jnp.int32, (1, LANES), 1) < NA_HEAD_DIM
    starts, offs = [], []
    for rr in range(NA_ROWS_PER_STEP):
        r = first_row + rr
        rs = jnp.clip(r - NA_WIN_ROWS // 2, 0, rows - NA_WIN_ROWS)
        starts.append(pl.multiple_of(rs * GRID_W, GRID_W))
        offs.append((NA_WIN_ROWS - 1) - (r - rs))
    work = [(rr, slab) for rr in range(NA_ROWS_PER_STEP) for slab in range(NA_SLABS)]

    def scores(i):
        rr, slab = work[i]
        cols = slice(slab * LANES, (slab + 1) * LANES)
        qs = q_ref[rr * GRID_W:(rr + 1) * GRID_W, cols]
        zero = jnp.zeros_like(qs)
        q2 = jnp.concatenate([jnp.where(first, qs, zero), jnp.where(first, zero, qs)], axis=0)
        s_bufs[i % 2][z] = _dot_nt(q2, k_ref[pl.ds(starts[rr], NA_BAND), cols])

    def finish(i):
        rr, slab = work[i]
        cols = slice(slab * LANES, (slab + 1) * LANES)
        bias = jnp.concatenate(
            [jnp.concatenate([bias_scr[2 * slab + e, offs[rr] + 2 * a] for a in range(NA_WIN_ROWS // 2)], axis=1)
             for e in range(2)], axis=0)
        s = s_bufs[i % 2][z] + bias
        m = jnp.max(s, axis=-1, keepdims=True)
        p = jnp.exp2(s - m)
        l = jnp.sum(p, axis=-1, keepdims=True)
        pv = _dot(p.astype(BF16), v_ref[pl.ds(starts[rr], NA_BAND), cols]) / l
        o_ref[rr * GRID_W:(rr + 1) * GRID_W, cols] = jnp.where(first, pv[:GRID_W], pv[GRID_W:]).astype(BF16)

    scores(0)
    for i in range(len(work)):
        if i + 1 < len(work):
            scores(i + 1)
        finish(i)


def _na_attn(qkvz, rpb_flat, batch, seq):
    hd = NA_HEADS * NA_HEAD_DIM
    steps = seq // GRID_W // NA_ROWS_PER_STEP
    tq = NA_ROWS_PER_STEP * GRID_W
    return pl.pallas_call(
        _na_attn_kernel,
        grid=(batch, steps),
        in_specs=[
            pl.BlockSpec(memory_space=pltpu.SMEM),
            pl.BlockSpec(memory_space=pltpu.SMEM),
            pl.BlockSpec((tq, hd), lambda b, r: (b * steps + r, 0)),
            pl.BlockSpec((seq, hd), lambda b, r: (b, 1)),
            pl.BlockSpec((seq, hd), lambda b, r: (b, 2)),
        ],
        out_specs=pl.BlockSpec((tq, hd), lambda b, r: (b * steps + r, 0)),
        out_shape=jax.ShapeDtypeStruct((batch * seq, hd), BF16),
        scratch_shapes=[pltpu.VMEM((NA_HEADS, NA_PAIR_OFFS, GRID_W, LANES), F32),
                        pltpu.VMEM((2, 2 * GRID_W, NA_BAND), F32),
                        pltpu.VMEM((2, 2 * GRID_W, NA_BAND), F32)],
        compiler_params=pltpu.CompilerParams(
            dimension_semantics=("arbitrary", "arbitrary"),
            vmem_limit_bytes=VMEM_LIMIT),
        name="na_attn",
    )(jnp.zeros((1,), jnp.int32), rpb_flat, qkvz, qkvz, qkvz)


def _post_kernel(x_ref, o_ref, z_ref, wo_ref, gp_ref, wg_ref, p_ref, wp_ref, gf_ref,
                 out_ref, *, feature_major, final_norm):
    if feature_major:
        z = z_ref[0].astype(F32)
        gated = (o_ref[0].astype(F32) * (z * _sigmoid(z))).astype(BF16)
        y = _dot_tn(gated, wo_ref[...])
    else:
        z = z_ref[...].astype(F32)
        gated = (o_ref[...].astype(F32) * (z * _sigmoid(z))).astype(BF16)
        y = _dot(gated, wo_ref[...])
    h = x_ref[...] + y
    hn = _rms(h, gp_ref[...]).astype(BF16)
    gate = _sigmoid(_dot(hn, wg_ref[...]))
    emb = _dot(p_ref[...].astype(BF16), wp_ref[...])
    xo = h + gate * emb
    if final_norm:
        xo = _rms(xo, gf_ref[...])
    out_ref[...] = xo


def _post_block(x2, o, z, z_col, w_out, g_ple, w_gate, p2, w_proj, g_final, *,
                feature_major, final_norm, seq):
    m = x2.shape[0]
    tm = TM_PROJ
    per_seq = seq // tm
    if feature_major:
        oz_block = (1, D_MODEL, tm)
        o_spec = pl.BlockSpec(oz_block, lambda i: (i // per_seq, 0, i % per_seq))
        z_spec = pl.BlockSpec(oz_block, lambda i: (i // per_seq, 0, i % per_seq))
    else:
        o_spec = pl.BlockSpec((tm, D_MODEL), lambda i: (i, 0))
        z_spec = pl.BlockSpec((tm, D_MODEL), lambda i: (i, z_col))
    full = lambda shape: pl.BlockSpec(shape, lambda i: (0,) * len(shape))
    kern = functools.partial(_post_kernel, feature_major=feature_major, final_norm=final_norm)
    return pl.pallas_call(
        kern,
        grid=(m // tm,),
        in_specs=[
            pl.BlockSpec((tm, D_MODEL), lambda i: (i, 0)),
            o_spec,
            z_spec,
            full((D_MODEL, D_MODEL)),
            full((1, D_MODEL)),
            full((D_MODEL, D_MODEL)),
            pl.BlockSpec((tm, PLE_DIM), lambda i: (i, 0)),
            full((PLE_DIM, D_MODEL)),
            full((1, D_MODEL)),
        ],
        out_specs=pl.BlockSpec((tm, D_MODEL), lambda i: (i, 0)),
        out_shape=jax.ShapeDtypeStruct((m, D_MODEL), F32),
        compiler_params=pltpu.CompilerParams(
            dimension_semantics=("arbitrary",),
            vmem_limit_bytes=VMEM_LIMIT),
        name="post_block_fm" if feature_major else "post_block",
    )(x2, o, z, w_out, g_ple, w_gate, p2, w_proj, g_final)


def _mla_inproj_kernel(x_ref, g_ref, wc_ref, wzt_ref, gq_ref, wq_ref, wqr_ref,
                       gkv_ref, wk_ref, wvt_ref, cq_ref, sq_ref, ck_ref, sk_ref,
                       qn_ref, qpe_ref, kn_ref, kpe_ref, vt_ref, zt_ref):
    xn = _rms(x_ref[...], g_ref[...]).astype(BF16)
    c = _dot(xn, wc_ref[...])
    zt_ref[0] = _dot_nt(wzt_ref[...], xn).astype(BF16)

    cq = _rms(c[:, :MLA_Q_RANK], gq_ref[...]).astype(BF16)
    ckv = _rms(c[:, MLA_Q_RANK:MLA_Q_RANK + MLA_KV_RANK], gkv_ref[...]).astype(BF16)
    off = MLA_Q_RANK + MLA_KV_RANK
    kpe = c[:, off:off + LANES] * ck_ref[...] + c[:, off + LANES:off + 2 * LANES] * sk_ref[...]
    kpe_ref[...] = kpe.astype(BF16)

    qt = _dot_nt(wq_ref[...], cq)
    qrot = _dot_nt(wqr_ref[...], cq)
    nope = MLA_HEADS * MLA_NOPE
    qn_ref[0] = qt[:nope].astype(BF16)
    qpe_ref[0] = (qt[nope:] * cq_ref[...] + qrot * sq_ref[...]).astype(BF16)

    kn_ref[...] = _dot(ckv, wk_ref[...]).astype(BF16)
    vt = _dot_nt(wvt_ref[...], ckv).astype(BF16)
    tm = vt.shape[1]
    pad_rows = lax.broadcasted_iota(jnp.int32, (MLA_V_PAD - MLA_V, MLA_BK), 0)
    ones_row = jnp.where(pad_rows == 0, 1.0, 0.0).astype(BF16)
    for h in range(MLA_HEADS):
        for s in range(tm // MLA_BK):
            vt_ref[0, h, s, :MLA_V] = vt[h * MLA_V:(h + 1) * MLA_V, s * MLA_BK:(s + 1) * MLA_BK]
            vt_ref[0, h, s, MLA_V:] = ones_row


def _mla_inproj(x2, g, wc, wzt, gq, wq, wqr, gkv, wk, wvt, cosq, sinq, cosk, sink, batch, seq):
    m = x2.shape[0]
    tm = TM_PROJ
    per_seq = seq // tm
    sub = tm // MLA_BK
    full = lambda a: pl.BlockSpec(a.shape, lambda i: (0,) * a.ndim)
    rowblk = lambda n: pl.BlockSpec((tm, n), lambda i: (i, 0))
    tabblk = lambda n: pl.BlockSpec((tm, n), lambda i: (i % per_seq, 0))
    fmtab = lambda n: pl.BlockSpec((n, tm), lambda i: (0, i % per_seq))
    fmblk = lambda n: pl.BlockSpec((1, n, tm), lambda i: (i // per_seq, 0, i % per_seq))
    nope = MLA_HEADS * MLA_NOPE
    pe = MLA_HEADS * MLA_ROPE
    return pl.pallas_call(
        _mla_inproj_kernel,
        grid=(m // tm,),
        in_specs=[rowblk(D_MODEL), full(g), full(wc), full(wzt), full(gq), full(wq), full(wqr),
                  full(gkv), full(wk), full(wvt),
                  fmtab(pe), fmtab(pe), tabblk(LANES), tabblk(LANES)],
        out_specs=[
            fmblk(nope), fmblk(pe), rowblk(nope), rowblk(LANES),
            pl.BlockSpec((1, MLA_HEADS, sub, MLA_V_PAD, MLA_BK),
                         lambda i: (i // per_seq, 0, i % per_seq, 0, 0)),
            pl.BlockSpec((1, D_MODEL, tm), lambda i: (i // per_seq, 0, i % per_seq)),
        ],
        out_shape=[
            jax.ShapeDtypeStruct((batch, nope, seq), BF16),
            jax.ShapeDtypeStruct((batch, pe, seq), BF16),
            jax.ShapeDtypeStruct((m, nope), BF16),
            jax.ShapeDtypeStruct((m, LANES), BF16),
            jax.ShapeDtypeStruct((batch, MLA_HEADS, seq // MLA_BK, MLA_V_PAD, MLA_BK), BF16),
            jax.ShapeDtypeStruct((batch, D_MODEL, seq), BF16),
        ],
        compiler_params=pltpu.CompilerParams(
            dimension_semantics=("arbitrary",),
            vmem_limit_bytes=VMEM_LIMIT),
        name="mla_inproj",
    )(x2, g, wc, wzt, gq, wq, wqr, gkv, wk, wvt, cosq, sinq, cosk, sink)


def _mla_attn_rescaling(q_scr, kn_ref, kpe_ref, vt_ref, ot_ref):
    nkb = kn_ref.shape[0] // MLA_BK
    for qb in range(MLA_QSTEPS):
        cols = slice(qb * MLA_BQ, (qb + 1) * MLA_BQ)

        def body(kb, carry, cols=cols):
            m, acc = carry
            off = pl.multiple_of(kb * MLA_BK, MLA_BK)
            k = jnp.concatenate([kn_ref[pl.ds(off, MLA_BK), :], kpe_ref[pl.ds(off, MLA_BK), :]], axis=1)
            st = _dot(k, q_scr[:, cols])
            m_new = jnp.maximum(m, jnp.max(st, axis=0, keepdims=True))
            p = jnp.exp2(st - m_new)
            acc = jnp.exp2(m - m_new) * acc + _dot(vt_ref[0, 0, kb], p.astype(BF16))
            return m_new, acc

        init = (jnp.full((1, MLA_BQ), -jnp.inf, F32), jnp.zeros((MLA_V_PAD, MLA_BQ), F32))
        _, acc = lax.fori_loop(0, nkb, body, init)
        ot_ref[0, :, cols] = (acc[:MLA_V] / acc[MLA_V:MLA_V + 1]).astype(BF16)


def _mla_attn_kernel(zero_ref, qn_ref, qpe_ref, kn_ref, kpe_ref, vt_ref, ot_ref, q_scr, s0_scr, s1_scr):
    h = pl.program_id(1)
    row = lax.broadcasted_iota(jnp.int32, (LANES, 1), 0)
    qn = qn_ref[0]
    qp = qpe_ref[0]
    q_scr[:LANES] = jnp.where(row // MLA_NOPE == h % (LANES // MLA_NOPE), qn, jnp.zeros_like(qn))
    q_scr[LANES:] = jnp.where(row // MLA_ROPE == h % (LANES // MLA_ROPE), qp, jnp.zeros_like(qp))
    nkb = kn_ref.shape[0] // MLA_BK

    z = zero_ref[0]
    s_bufs = (s0_scr, s1_scr)
    blocks = [(qb, kb) for qb in range(MLA_QSTEPS) for kb in range(nkb)]

    def scores(i):
        qb, kb = blocks[i]
        rows = slice(kb * MLA_BK, (kb + 1) * MLA_BK)
        k = jnp.concatenate([kn_ref[rows, :], kpe_ref[rows, :]], axis=1)
        s_bufs[i % 2][z] = _dot(k, q_scr[:, qb * MLA_BQ:(qb + 1) * MLA_BQ])

    finite = None
    scores(0)
    for i, (qb, kb) in enumerate(blocks):
        if i + 1 < len(blocks):
            scores(i + 1)
        if kb == 0:
            m_ref = jnp.max(s_bufs[i % 2][z], axis=0, keepdims=True)
            acc = jnp.zeros((MLA_V_PAD, MLA_BQ), F32)
        acc = acc + _dot(vt_ref[0, 0, kb], jnp.exp2(s_bufs[i % 2][z] - m_ref).astype(BF16))
        if kb == nkb - 1:
            denom = acc[MLA_V:MLA_V + 1]
            ot_ref[0, :, qb * MLA_BQ:(qb + 1) * MLA_BQ] = (acc[:MLA_V] / denom).astype(BF16)
            ok = jnp.max(denom) < MLA_DENOM_LIMIT
            finite = ok if finite is None else jnp.logical_and(finite, ok)

    @pl.when(jnp.logical_not(finite))
    def _():
        _mla_attn_rescaling(q_scr, kn_ref, kpe_ref, vt_ref, ot_ref)


def _mla_attn(qn, qpe, kn, kpe, vt, batch, seq):
    bq = MLA_QSTEPS * MLA_BQ
    per_n = LANES // MLA_NOPE
    per_r = LANES // MLA_ROPE
    return pl.pallas_call(
        _mla_attn_kernel,
        grid=(batch, MLA_HEADS, seq // bq),
        in_specs=[
            pl.BlockSpec(memory_space=pltpu.SMEM),
            pl.BlockSpec((1, LANES, bq), lambda b, h, i: (b, h // per_n, i)),
            pl.BlockSpec((1, LANES, bq), lambda b, h, i: (b, h // per_r, i)),
            pl.BlockSpec((seq, LANES), lambda b, h, i: (b, h // per_n)),
            pl.BlockSpec((seq, LANES), lambda b, h, i: (b, 0)),
            pl.BlockSpec((1, 1, seq // MLA_BK, MLA_V_PAD, MLA_BK), lambda b, h, i: (b, h, 0, 0, 0)),
        ],
        out_specs=pl.BlockSpec((1, MLA_V, bq), lambda b, h, i: (b, h, i)),
        out_shape=jax.ShapeDtypeStruct((batch, MLA_HEADS * MLA_V, seq), BF16),
        scratch_shapes=[pltpu.VMEM((2 * LANES, bq), BF16),
                        pltpu.VMEM((2, MLA_BK, MLA_BQ), F32),
                        pltpu.VMEM((2, MLA_BK, MLA_BQ), F32)],
        compiler_params=pltpu.CompilerParams(
            dimension_semantics=("arbitrary", "arbitrary", "arbitrary"),
            vmem_limit_bytes=VMEM_LIMIT),
        name="mla_attn",
    )(jnp.zeros((1,), jnp.int32), qn, qpe, kn, kpe, vt)


def _rot_cols(w):
    half = MLA_ROPE // 2
    return jnp.concatenate([-w[..., half:], w[..., :half]], axis=-1)


def _rope_tables(seq):
    inv = 1.0 / (ROPE_BASE ** (jnp.arange(0, MLA_ROPE, 2, dtype=F32) / MLA_ROPE))
    ang = jnp.arange(seq, dtype=F32)[:, None] * inv[None, :]
    cos = jnp.concatenate([jnp.cos(ang), jnp.cos(ang)], axis=-1)
    sin = jnp.concatenate([jnp.sin(ang), jnp.sin(ang)], axis=-1)
    return cos, sin


def kernel(x, p, norm_g, na_w_in, na_rpb, na_w_out, mla_w_in, mla_q_norm, mla_w_qb,
           mla_kv_norm, mla_w_kvb, mla_w_out, ple_norm, ple_w_gate, ple_w_proj, final_norm):
    batch, seq, d = x.shape
    m = batch * seq
    x2 = x.reshape(m, d)
    row = lambda v: v.reshape(1, -1).astype(F32)

    hd = NA_HEADS * NA_HEAD_DIM
    w_in0 = na_w_in[0]
    w_in0 = jnp.concatenate([w_in0[:, :hd] * (NA_HEAD_DIM ** -0.5 * LOG2E), w_in0[:, hd:]], axis=1)
    qkvz = _na_inproj(x2, row(norm_g[0]), w_in0.astype(BF16))
    o0 = _na_attn(qkvz, na_rpb[0].reshape(-1).astype(F32), batch, seq)
    x2 = _post_block(x2, o0, qkvz, 3, na_w_out[0].astype(BF16), row(ple_norm[0]),
                     ple_w_gate[0].astype(BF16), p[0].reshape(m, PLE_DIM),
                     ple_w_proj[0].astype(BF16), row(final_norm),
                     feature_major=False, final_norm=False, seq=seq)

    w_in1 = mla_w_in[0]
    o1, o2 = MLA_Q_RANK, MLA_Q_RANK + MLA_KV_RANK
    o3 = o2 + MLA_ROPE
    w_kr = w_in1[:, o2:o3]
    rep = LANES // MLA_ROPE
    wc = jnp.concatenate([w_in1[:, :o2], jnp.tile(w_kr, (1, rep)),
                          jnp.tile(_rot_cols(w_kr), (1, rep))], axis=1).astype(BF16)
    wzt = w_in1[:, o3:].T.astype(BF16)

    scale = (MLA_NOPE + MLA_ROPE) ** -0.5 * LOG2E
    wq3 = mla_w_qb[0].reshape(MLA_Q_RANK, MLA_HEADS, MLA_NOPE + MLA_ROPE) * scale
    wq_n = wq3[:, :, :MLA_NOPE].reshape(MLA_Q_RANK, -1)
    wq_p = wq3[:, :, MLA_NOPE:]
    wq = jnp.concatenate([wq_n, wq_p.reshape(MLA_Q_RANK, -1)], axis=1).T.astype(BF16)
    wqr = _rot_cols(wq_p).reshape(MLA_Q_RANK, -1).T.astype(BF16)

    wkv3 = mla_w_kvb[0].reshape(MLA_KV_RANK, MLA_HEADS, MLA_NOPE + MLA_V)
    wk = wkv3[:, :, :MLA_NOPE].reshape(MLA_KV_RANK, -1).astype(BF16)
    wvt = wkv3[:, :, MLA_NOPE:].reshape(MLA_KV_RANK, -1).T.astype(BF16)

    cos, sin = _rope_tables(seq)
    cosq, sinq = jnp.tile(cos, (1, MLA_HEADS)).T, jnp.tile(sin, (1, MLA_HEADS)).T
    cosk, sink = jnp.tile(cos, (1, rep)), jnp.tile(sin, (1, rep))

    qn, qpe, kn, kpe, vt, zt = _mla_inproj(
        x2, row(norm_g[1]), wc, wzt, row(mla_q_norm[0]), wq, wqr, row(mla_kv_norm[0]),
        wk, wvt, cosq, sinq, cosk, sink, batch, seq)
    ot = _mla_attn(qn, qpe, kn, kpe, vt, batch, seq)
    out = _post_block(x2, ot, zt, 0, mla_w_out[0].astype(BF16), row(ple_norm[1]),
                      ple_w_gate[1].astype(BF16), p[1].reshape(m, PLE_DIM),
                      ple_w_proj[1].astype(BF16), row(final_norm),
                      feature_major=True, final_norm=True, seq=seq)
    return out.reshape(batch, seq, d)
```

```python
import functools

import jax
import jax.numpy as jnp
from jax import lax
from jax.experimental import pallas as pl
from jax.experimental.pallas import tpu as pltpu

D_MODEL = 1024
GRID_W = 64
NA_WIN_ROWS = 8
NA_WIN_COLS = 16
NA_HEADS = 16
NA_HEAD_DIM = 64
MLA_HEADS = 16
MLA_Q_RANK = 384
MLA_KV_RANK = 256
MLA_NOPE = 64
MLA_ROPE = 32
MLA_V = 64
MLA_V_PAD = 80
ROPE_BASE = 10000.0
PLE_DIM = 256
EPS = 1e-6

LANES = 128
NEG_BIG = -1e30
VMEM_LIMIT = 56 * 1024 * 1024

TM_PROJ = 512
MLA_BQ = 512
MLA_QSTEPS = 4
MLA_BK = 512
MLA_DENOM_LIMIT = 3e38
LOG2E = 1.4426950408889634

BF16 = jnp.bfloat16
F32 = jnp.float32


def _rms(x, g):
    ms = jnp.mean(x * x, axis=-1, keepdims=True)
    return x * lax.rsqrt(ms + EPS) * g


def _sigmoid(x):
    return 1.0 / (1.0 + jnp.exp(-x))


def _dot(a, b):
    return jnp.dot(a, b, preferred_element_type=F32)


def _dot_nt(a, b):
    return lax.dot_general(a, b, (((1,), (1,)), ((), ())), preferred_element_type=F32)


def _dot_tn(a, b):
    return lax.dot_general(a, b, (((0,), (0,)), ((), ())), preferred_element_type=F32)


def _na_inproj_kernel(x_ref, g_ref, w_ref, o_ref, xn_ref):
    @pl.when(pl.program_id(1) == 0)
    def _():
        xn_ref[...] = _rms(x_ref[...], g_ref[...]).astype(BF16)

    o_ref[...] = _dot(xn_ref[...], w_ref[...]).astype(BF16)


def _na_inproj(x2, g, w):
    m = x2.shape[0]
    n = w.shape[1]
    tn = n
    return pl.pallas_call(
        _na_inproj_kernel,
        grid=(m // TM_PROJ, n // tn),
        in_specs=[
            pl.BlockSpec((TM_PROJ, D_MODEL), lambda i, j: (i, 0)),
            pl.BlockSpec((1, D_MODEL), lambda i, j: (0, 0)),
            pl.BlockSpec((D_MODEL, tn), lambda i, j: (0, j)),
        ],
        out_specs=pl.BlockSpec((TM_PROJ, tn), lambda i, j: (i, j)),
        out_shape=jax.ShapeDtypeStruct((m, n), BF16),
        scratch_shapes=[pltpu.VMEM((TM_PROJ, D_MODEL), BF16)],
        compiler_params=pltpu.CompilerParams(
            dimension_semantics=("arbitrary", "arbitrary"),
            vmem_limit_bytes=VMEM_LIMIT),
        name="na_inproj",
    )(x2, g, w)


NA_BAND = NA_WIN_ROWS * GRID_W
NA_SLABS = NA_HEADS * NA_HEAD_DIM // LANES
NA_ROW_OFFS = 2 * NA_WIN_ROWS - 1
NA_COL_OFFS = 2 * NA_WIN_COLS - 1
NA_PAIR_OFFS = NA_ROW_OFFS - 1
NA_ROWS_PER_STEP = 4


def _na_build_bias(rpb_ref, bias_scr):
    c = lax.broadcasted_iota(jnp.int32, (GRID_W, LANES), 0)
    lane = lax.broadcasted_iota(jnp.int32, (GRID_W, LANES), 1)
    j = lane % GRID_W
    cs = jnp.clip(c - NA_WIN_COLS // 2, 0, GRID_W - NA_WIN_COLS)
    valid = (j >= cs) & (j < cs + NA_WIN_COLS)
    diag = jnp.where(valid, j - c + (NA_WIN_COLS - 1), -1)
    second = lax.broadcasted_iota(jnp.int32, (1, LANES), 1) >= GRID_W

    def tile(idx, carry):
        h = idx // NA_PAIR_OFFS
        o = idx % NA_PAIR_OFFS
        base = (h * NA_ROW_OFFS + o) * NA_COL_OFFS
        acc = jnp.full((GRID_W, LANES), NEG_BIG, F32)
        for k in range(NA_COL_OFFS):
            val = jnp.where(second, rpb_ref[base + NA_COL_OFFS + k], rpb_ref[base + k]) * LOG2E
            acc = jnp.where(diag == k, val, acc)
        bias_scr[h, o] = acc
        return carry

    lax.fori_loop(0, NA_HEADS * NA_PAIR_OFFS, tile, 0)


def _na_attn_kernel(zero_ref, rpb_ref, q_ref, k_ref, v_ref, o_ref, bias_scr, s0_scr, s1_scr):
    first_row = pl.program_id(1) * NA_ROWS_PER_STEP

    @pl.when((pl.program_id(0) == 0) & (first_row == 0))
    def _():
        _na_build_bias(rpb_ref, bias_scr)

    z = zero_ref[0]
    s_bufs = (s0_scr, s1_scr)
    rows = k_ref.shape[0] // GRID_W
    first = lax.broadcasted_iota(jnp.int32, (1, LANES), 1) < NA_HEAD_DIM
    starts, offs = [], []
    for rr in range(NA_ROWS_PER_STEP):
        r = first_row + rr
        rs = jnp.clip(r - NA_WIN_ROWS // 2, 0, rows - NA_WIN_ROWS)
        starts.append(pl.multiple_of(rs * GRID_W, GRID_W))
        offs.append((NA_WIN_ROWS - 1) - (r - rs))
    work = [(rr, slab) for rr in range(NA_ROWS_PER_STEP) for slab in range(NA_SLABS)]

    def scores(i):
        rr, slab = work[i]
        cols = slice(slab * LANES, (slab + 1) * LANES)
        qs = q_ref[rr * GRID_W:(rr + 1) * GRID_W, cols]
        zero = jnp.zeros_like(qs)
        q2 = jnp.concatenate([jnp.where(first, qs, zero), jnp.where(first, zero, qs)], axis=0)
        s_bufs[i % 2][z] = _dot_nt(q2, k_ref[pl.ds(starts[rr], NA_BAND), cols])

    def finish(i):
        rr, slab = work[i]
        cols = slice(slab * LANES, (slab + 1) * LANES)
        bias = jnp.concatenate(
            [jnp.concatenate([bias_scr[2 * slab + e, offs[rr] + 2 * a] for a in range(NA_WIN_ROWS // 2)], axis=1)
             for e in range(2)], axis=0)
        s = s_bufs[i % 2][z] + bias
        m = jnp.max(s, axis=-1, keepdims=True)
        p = jnp.exp2(s - m)
        l = jnp.sum(p, axis=-1, keepdims=True)
        pv = _dot(p.astype(BF16), v_ref[pl.ds(starts[rr], NA_BAND), cols]) / l
        o_ref[rr * GRID_W:(rr + 1) * GRID_W, cols] = jnp.where(first, pv[:GRID_W], pv[GRID_W:]).astype(BF16)

    scores(0)
    for i in range(len(work)):
        if i + 1 < len(work):
            scores(i + 1)
        finish(i)


def _na_attn(qkvz, rpb_flat, batch, seq):
    hd = NA_HEADS * NA_HEAD_DIM
    steps = seq // GRID_W // NA_ROWS_PER_STEP
    tq = NA_ROWS_PER_STEP * GRID_W
    return pl.pallas_call(
        _na_attn_kernel,
        grid=(batch, steps),
        in_specs=[
            pl.BlockSpec(memory_space=pltpu.SMEM),
            pl.BlockSpec(memory_space=pltpu.SMEM),
            pl.BlockSpec((tq, hd), lambda b, r: (b * steps + r, 0)),
            pl.BlockSpec((seq, hd), lambda b, r: (b, 1)),
            pl.BlockSpec((seq, hd), lambda b, r: (b, 2)),
        ],
        out_specs=pl.BlockSpec((tq, hd), lambda b, r: (b * steps + r, 0)),
        out_shape=jax.ShapeDtypeStruct((batch * seq, hd), BF16),
        scratch_shapes=[pltpu.VMEM((NA_HEADS, NA_PAIR_OFFS, GRID_W, LANES), F32),
                        pltpu.VMEM((2, 2 * GRID_W, NA_BAND), F32),
                        pltpu.VMEM((2, 2 * GRID_W, NA_BAND), F32)],
        compiler_params=pltpu.CompilerParams(
            dimension_semantics=("arbitrary", "arbitrary"),
            vmem_limit_bytes=VMEM_LIMIT),
        name="na_attn",
    )(jnp.zeros((1,), jnp.int32), rpb_flat, qkvz, qkvz, qkvz)


def _post_kernel(x_ref, o_ref, z_ref, wo_ref, gp_ref, wg_ref, p_ref, wp_ref, gf_ref,
                 out_ref, *, feature_major, final_norm):
    if feature_major:
        z = z_ref[0].astype(F32)
        gated = (o_ref[0].astype(F32) * (z * _sigmoid(z))).astype(BF16)
        y = _dot_tn(gated, wo_ref[...])
    else:
        z = z_ref[...].astype(F32)
        gated = (o_ref[...].astype(F32) * (z * _sigmoid(z))).astype(BF16)
        y = _dot(gated, wo_ref[...])
    h = x_ref[...] + y
    hn = _rms(h, gp_ref[...]).astype(BF16)
    gate = _sigmoid(_dot(hn, wg_ref[...]))
    emb = _dot(p_ref[0].astype(BF16), wp_ref[...])
    xo = h + gate * emb
    if final_norm:
        xo = _rms(xo, gf_ref[...])
    out_ref[...] = xo


def _post_block(x2, o, z, z_col, w_out, g_ple, w_gate, p3, layer, w_proj, g_final, *,
                feature_major, final_norm, seq):
    m = x2.shape[0]
    tm = TM_PROJ
    per_seq = seq // tm
    if feature_major:
        oz_block = (1, D_MODEL, tm)
        o_spec = pl.BlockSpec(oz_block, lambda i: (i // per_seq, 0, i % per_seq))
        z_spec = pl.BlockSpec(oz_block, lambda i: (i // per_seq, 0, i % per_seq))
    else:
        o_spec = pl.BlockSpec((tm, D_MODEL), lambda i: (i, 0))
        z_spec = pl.BlockSpec((tm, D_MODEL), lambda i: (i, z_col))
    full = lambda shape: pl.BlockSpec(shape, lambda i: (0,) * len(shape))
    kern = functools.partial(_post_kernel, feature_major=feature_major, final_norm=final_norm)
    return pl.pallas_call(
        kern,
        grid=(m // tm,),
        in_specs=[
            pl.BlockSpec((tm, D_MODEL), lambda i: (i, 0)),
            o_spec,
            z_spec,
            full((D_MODEL, D_MODEL)),
            full((1, D_MODEL)),
            full((D_MODEL, D_MODEL)),
            pl.BlockSpec((1, tm, PLE_DIM), lambda i: (layer, i, 0)),
            full((PLE_DIM, D_MODEL)),
            full((1, D_MODEL)),
        ],
        out_specs=pl.BlockSpec((tm, D_MODEL), lambda i: (i, 0)),
        out_shape=jax.ShapeDtypeStruct((m, D_MODEL), F32),
        compiler_params=pltpu.CompilerParams(
            dimension_semantics=("arbitrary",),
            vmem_limit_bytes=VMEM_LIMIT),
        name="post_block_fm" if feature_major else "post_block",
    )(x2, o, z, w_out, g_ple, w_gate, p3, w_proj, g_final)


def _mla_inproj_kernel(x_ref, g_ref, wc_ref, wzt_ref, gq_ref, wq_ref, wqr_ref,
                       gkv_ref, wk_ref, wvt_ref, cq_ref, sq_ref, ck_ref, sk_ref,
                       qn_ref, qpe_ref, kn_ref, kpe_ref, vt_ref, zt_ref):
    xn = _rms(x_ref[...], g_ref[...]).astype(BF16)
    c = _dot(xn, wc_ref[...])
    zt_ref[0] = _dot_nt(wzt_ref[...], xn).astype(BF16)

    cq = _rms(c[:, :MLA_Q_RANK], gq_ref[...]).astype(BF16)
    ckv = _rms(c[:, MLA_Q_RANK:MLA_Q_RANK + MLA_KV_RANK], gkv_ref[...]).astype(BF16)
    off = MLA_Q_RANK + MLA_KV_RANK
    kpe = c[:, off:off + LANES] * ck_ref[...] + c[:, off + LANES:off + 2 * LANES] * sk_ref[...]
    kpe_ref[...] = kpe.astype(BF16)

    qt = _dot_nt(wq_ref[...], cq)
    qrot = _dot_nt(wqr_ref[...], cq)
    nope = MLA_HEADS * MLA_NOPE
    qn_ref[0] = qt[:nope].astype(BF16)
    cos_q = jnp.tile(cq_ref[...], (MLA_HEADS, 1))
    sin_q = jnp.tile(sq_ref[...], (MLA_HEADS, 1))
    qpe_ref[0] = (qt[nope:] * cos_q + qrot * sin_q).astype(BF16)

    kn_ref[...] = _dot(ckv, wk_ref[...]).astype(BF16)
    vt = _dot_nt(wvt_ref[...], ckv).astype(BF16)
    tm = vt.shape[1]
    pad_rows = lax.broadcasted_iota(jnp.int32, (MLA_V_PAD - MLA_V, MLA_BK), 0)
    ones_row = jnp.where(pad_rows == 0, 1.0, 0.0).astype(BF16)
    for h in range(MLA_HEADS):
        for s in range(tm // MLA_BK):
            vt_ref[0, h, s, :MLA_V] = vt[h * MLA_V:(h + 1) * MLA_V, s * MLA_BK:(s + 1) * MLA_BK]
            vt_ref[0, h, s, MLA_V:] = ones_row


def _mla_inproj(x2, g, wc, wzt, gq, wq, wqr, gkv, wk, wvt, cosq, sinq, cosk, sink, batch, seq):
    m = x2.shape[0]
    tm = TM_PROJ
    per_seq = seq // tm
    sub = tm // MLA_BK
    full = lambda a: pl.BlockSpec(a.shape, lambda i: (0,) * a.ndim)
    rowblk = lambda n: pl.BlockSpec((tm, n), lambda i: (i, 0))
    tabblk = lambda n: pl.BlockSpec((tm, n), lambda i: (i % per_seq, 0))
    fmtab = lambda n: pl.BlockSpec((n, tm), lambda i: (0, i % per_seq))
    fmblk = lambda n: pl.BlockSpec((1, n, tm), lambda i: (i // per_seq, 0, i % per_seq))
    nope = MLA_HEADS * MLA_NOPE
    pe = MLA_HEADS * MLA_ROPE
    return pl.pallas_call(
        _mla_inproj_kernel,
        grid=(m // tm,),
        in_specs=[rowblk(D_MODEL), full(g), full(wc), full(wzt), full(gq), full(wq), full(wqr),
                  full(gkv), full(wk), full(wvt),
                  fmtab(MLA_ROPE), fmtab(MLA_ROPE), tabblk(LANES), tabblk(LANES)],
        out_specs=[
            fmblk(nope), fmblk(pe), rowblk(nope), rowblk(LANES),
            pl.BlockSpec((1, MLA_HEADS, sub, MLA_V_PAD, MLA_BK),
                         lambda i: (i // per_seq, 0, i % per_seq, 0, 0)),
            pl.BlockSpec((1, D_MODEL, tm), lambda i: (i // per_seq, 0, i % per_seq)),
        ],
        out_shape=[
            jax.ShapeDtypeStruct((batch, nope, seq), BF16),
            jax.ShapeDtypeStruct((batch, pe, seq), BF16),
            jax.ShapeDtypeStruct((m, nope), BF16),
            jax.ShapeDtypeStruct((m, LANES), BF16),
            jax.ShapeDtypeStruct((batch, MLA_HEADS, seq // MLA_BK, MLA_V_PAD, MLA_BK), BF16),
            jax.ShapeDtypeStruct((batch, D_MODEL, seq), BF16),
        ],
        compiler_params=pltpu.CompilerParams(
            dimension_semantics=("arbitrary",),
            vmem_limit_bytes=VMEM_LIMIT),
        name="mla_inproj",
    )(x2, g, wc, wzt, gq, wq, wqr, gkv, wk, wvt, cosq, sinq, cosk, sink)


def _mla_attn_rescaling(q_scr, kn_ref, kpe_ref, vt_ref, ot_ref):
    nkb = kn_ref.shape[0] // MLA_BK
    for qb in range(MLA_QSTEPS):
        cols = slice(qb * MLA_BQ, (qb + 1) * MLA_BQ)

        def body(kb, carry, cols=cols):
            m, acc = carry
            off = pl.multiple_of(kb * MLA_BK, MLA_BK)
            k = jnp.concatenate([kn_ref[pl.ds(off, MLA_BK), :], kpe_ref[pl.ds(off, MLA_BK), :]], axis=1)
            st = _dot(k, q_scr[:, cols])
            m_new = jnp.maximum(m, jnp.max(st, axis=0, keepdims=True))
            p = jnp.exp2(st - m_new)
            acc = jnp.exp2(m - m_new) * acc + _dot(vt_ref[0, 0, kb], p.astype(BF16))
            return m_new, acc

        init = (jnp.full((1, MLA_BQ), -jnp.inf, F32), jnp.zeros((MLA_V_PAD, MLA_BQ), F32))
        _, acc = lax.fori_loop(0, nkb, body, init)
        ot_ref[0, :, cols] = (acc[:MLA_V] / acc[MLA_V:MLA_V + 1]).astype(BF16)


def _mla_attn_kernel(zero_ref, qn_ref, qpe_ref, kn_ref, kpe_ref, vt_ref, ot_ref, q_scr, s0_scr, s1_scr):
    h = pl.program_id(1)
    row = lax.broadcasted_iota(jnp.int32, (LANES, 1), 0)
    qn = qn_ref[0]
    qp = qpe_ref[0]
    q_scr[:LANES] = jnp.where(row // MLA_NOPE == h % (LANES // MLA_NOPE), qn, jnp.zeros_like(qn))
    q_scr[LANES:] = jnp.where(row // MLA_ROPE == h % (LANES // MLA_ROPE), qp, jnp.zeros_like(qp))
    nkb = kn_ref.shape[0] // MLA_BK

    z = zero_ref[0]
    s_bufs = (s0_scr, s1_scr)
    blocks = [(qb, kb) for qb in range(MLA_QSTEPS) for kb in range(nkb)]

    def scores(i):
        qb, kb = blocks[i]
        rows = slice(kb * MLA_BK, (kb + 1) * MLA_BK)
        k = jnp.concatenate([kn_ref[rows, :], kpe_ref[rows, :]], axis=1)
        s_bufs[i % 2][z] = _dot(k, q_scr[:, qb * MLA_BQ:(qb + 1) * MLA_BQ])

    finite = None
    scores(0)
    for i, (qb, kb) in enumerate(blocks):
        if i + 1 < len(blocks):
            scores(i + 1)
        if kb == 0:
            m_ref = jnp.max(s_bufs[i % 2][z], axis=0, keepdims=True)
            acc = jnp.zeros((MLA_V_PAD, MLA_BQ), F32)
        acc = acc + _dot(vt_ref[0, 0, kb], jnp.exp2(s_bufs[i % 2][z] - m_ref).astype(BF16))
        if kb == nkb - 1:
            denom = acc[MLA_V:MLA_V + 1]
            ot_ref[0, :, qb * MLA_BQ:(qb + 1) * MLA_BQ] = (acc[:MLA_V] / denom).astype(BF16)
            ok = jnp.max(denom) < MLA_DENOM_LIMIT
            finite = ok if finite is None else jnp.logical_and(finite, ok)

    @pl.when(jnp.logical_not(finite))
    def _():
        _mla_attn_rescaling(q_scr, kn_ref, kpe_ref, vt_ref, ot_ref)


def _mla_attn(qn, qpe, kn, kpe, vt, batch, seq):
    bq = MLA_QSTEPS * MLA_BQ
    per_n = LANES // MLA_NOPE
    per_r = LANES // MLA_ROPE
    return pl.pallas_call(
        _mla_attn_kernel,
        grid=(batch, MLA_HEADS, seq // bq),
        in_specs=[
            pl.BlockSpec(memory_space=pltpu.SMEM),
            pl.BlockSpec((1, LANES, bq), lambda b, h, i: (b, h // per_n, i)),
            pl.BlockSpec((1, LANES, bq), lambda b, h, i: (b, h // per_r, i)),
            pl.BlockSpec((seq, LANES), lambda b, h, i: (b, h // per_n)),
            pl.BlockSpec((seq, LANES), lambda b, h, i: (b, 0)),
            pl.BlockSpec((1, 1, seq // MLA_BK, MLA_V_PAD, MLA_BK), lambda b, h, i: (b, h, 0, 0, 0)),
        ],
        out_specs=pl.BlockSpec((1, MLA_V, bq), lambda b, h, i: (b, h, i)),
        out_shape=jax.ShapeDtypeStruct((batch, MLA_HEADS * MLA_V, seq), BF16),
        scratch_shapes=[pltpu.VMEM((2 * LANES, bq), BF16),
                        pltpu.VMEM((2, MLA_BK, MLA_BQ), F32),
                        pltpu.VMEM((2, MLA_BK, MLA_BQ), F32)],
        compiler_params=pltpu.CompilerParams(
            dimension_semantics=("arbitrary", "arbitrary", "arbitrary"),
            vmem_limit_bytes=VMEM_LIMIT),
        name="mla_attn",
    )(jnp.zeros((1,), jnp.int32), qn, qpe, kn, kpe, vt)


def _rot_cols(w):
    half = MLA_ROPE // 2
    return jnp.concatenate([-w[..., half:], w[..., :half]], axis=-1)


def _rope_tables(seq):
    inv = 1.0 / (ROPE_BASE ** (jnp.arange(0, MLA_ROPE, 2, dtype=F32) / MLA_ROPE))
    ang = jnp.arange(seq, dtype=F32)[:, None] * inv[None, :]
    cos = jnp.concatenate([jnp.cos(ang), jnp.cos(ang)], axis=-1)
    sin = jnp.concatenate([jnp.sin(ang), jnp.sin(ang)], axis=-1)
    return cos, sin


def kernel(x, p, norm_g, na_w_in, na_rpb, na_w_out, mla_w_in, mla_q_norm, mla_w_qb,
           mla_kv_norm, mla_w_kvb, mla_w_out, ple_norm, ple_w_gate, ple_w_proj, final_norm):
    batch, seq, d = x.shape
    m = batch * seq
    x2 = x.reshape(m, d)
    p3 = p.reshape(p.shape[0], m, PLE_DIM)
    row = lambda v: v.reshape(1, -1).astype(F32)

    hd = NA_HEADS * NA_HEAD_DIM
    w_in0 = na_w_in[0]
    w_in0 = jnp.concatenate([w_in0[:, :hd] * (NA_HEAD_DIM ** -0.5 * LOG2E), w_in0[:, hd:]], axis=1)
    qkvz = _na_inproj(x2, row(norm_g[0]), w_in0.astype(BF16))
    o0 = _na_attn(qkvz, na_rpb[0].reshape(-1).astype(F32), batch, seq)
    x2 = _post_block(x2, o0, qkvz, 3, na_w_out[0].astype(BF16), row(ple_norm[0]),
                     ple_w_gate[0].astype(BF16), p3, 0,
                     ple_w_proj[0].astype(BF16), row(final_norm),
                     feature_major=False, final_norm=False, seq=seq)

    w_in1 = mla_w_in[0]
    o1, o2 = MLA_Q_RANK, MLA_Q_RANK + MLA_KV_RANK
    o3 = o2 + MLA_ROPE
    w_kr = w_in1[:, o2:o3]
    rep = LANES // MLA_ROPE
    wc = jnp.concatenate([w_in1[:, :o2], jnp.tile(w_kr, (1, rep)),
                          jnp.tile(_rot_cols(w_kr), (1, rep))], axis=1).astype(BF16)
    wzt = w_in1[:, o3:].T.astype(BF16)

    scale = (MLA_NOPE + MLA_ROPE) ** -0.5 * LOG2E
    wq3 = mla_w_qb[0].reshape(MLA_Q_RANK, MLA_HEADS, MLA_NOPE + MLA_ROPE) * scale
    wq_n = wq3[:, :, :MLA_NOPE].reshape(MLA_Q_RANK, -1)
    wq_p = wq3[:, :, MLA_NOPE:]
    wq = jnp.concatenate([wq_n, wq_p.reshape(MLA_Q_RANK, -1)], axis=1).T.astype(BF16)
    wqr = _rot_cols(wq_p).reshape(MLA_Q_RANK, -1).T.astype(BF16)

    wkv3 = mla_w_kvb[0].reshape(MLA_KV_RANK, MLA_HEADS, MLA_NOPE + MLA_V)
    wk = wkv3[:, :, :MLA_NOPE].reshape(MLA_KV_RANK, -1).astype(BF16)
    wvt = wkv3[:, :, MLA_NOPE:].reshape(MLA_KV_RANK, -1).T.astype(BF16)

    cos, sin = _rope_tables(seq)
    cosq, sinq = cos.T, sin.T
    cosk, sink = jnp.tile(cos, (1, rep)), jnp.tile(sin, (1, rep))

    qn, qpe, kn, kpe, vt, zt = _mla_inproj(
        x2, row(norm_g[1]), wc, wzt, row(mla_q_norm[0]), wq, wqr, row(mla_kv_norm[0]),
        wk, wvt, cosq, sinq, cosk, sink, batch, seq)
    ot = _mla_attn(qn, qpe, kn, kpe, vt, batch, seq)
    out = _post_block(x2, ot, zt, 0, mla_w_out[0].astype(BF16), row(ple_norm[1]),
                      ple_w_gate[1].astype(BF16), p3, 1,
                      ple_w_proj[1].astype(BF16), row(final_norm),
                      feature_major=True, final_norm=True, seq=seq)
    return out.reshape(batch, seq, d)
```

```python
import functools

import jax
import jax.numpy as jnp
from jax import lax
from jax.experimental import pallas as pl
from jax.experimental.pallas import tpu as pltpu

D_MODEL = 1024
GRID_W = 64
NA_WIN_ROWS = 8
NA_WIN_COLS = 16
NA_HEADS = 16
NA_HEAD_DIM = 64
MLA_HEADS = 16
MLA_Q_RANK = 384
MLA_KV_RANK = 256
MLA_NOPE = 64
MLA_ROPE = 32
MLA_V = 64
MLA_V_PAD = 80
ROPE_BASE = 10000.0
PLE_DIM = 256
EPS = 1e-6

LANES = 128
NEG_BIG = -1e30
VMEM_LIMIT = 56 * 1024 * 1024

TM_PROJ = 512
MLA_BQ = 512
MLA_QSTEPS = 4
MLA_BK = 512
MLA_DENOM_LIMIT = 3e38
LOG2E = 1.4426950408889634

BF16 = jnp.bfloat16
F32 = jnp.float32


def _rms(x, g):
    ms = jnp.mean(x * x, axis=-1, keepdims=True)
    return x * lax.rsqrt(ms + EPS) * g


def _sigmoid(x):
    return 1.0 / (1.0 + jnp.exp(-x))


def _dot(a, b):
    return jnp.dot(a, b, preferred_element_type=F32)


def _dot_nt(a, b):
    return lax.dot_general(a, b, (((1,), (1,)), ((), ())), preferred_element_type=F32)


def _dot_tn(a, b):
    return lax.dot_general(a, b, (((0,), (0,)), ((), ())), preferred_element_type=F32)


def _na_inproj_kernel(x_ref, g_ref, w_ref, o_ref, xn_ref):
    @pl.when(pl.program_id(1) == 0)
    def _():
        xn_ref[...] = _rms(x_ref[...], g_ref[...]).astype(BF16)

    o_ref[...] = _dot(xn_ref[...], w_ref[...]).astype(BF16)


def _na_inproj(x2, g, w):
    m = x2.shape[0]
    n = w.shape[1]
    tn = n
    return pl.pallas_call(
        _na_inproj_kernel,
        grid=(m // TM_PROJ, n // tn),
        in_specs=[
            pl.BlockSpec((TM_PROJ, D_MODEL), lambda i, j: (i, 0)),
            pl.BlockSpec((1, D_MODEL), lambda i, j: (0, 0)),
            pl.BlockSpec((D_MODEL, tn), lambda i, j: (0, j)),
        ],
        out_specs=pl.BlockSpec((TM_PROJ, tn), lambda i, j: (i, j)),
        out_shape=jax.ShapeDtypeStruct((m, n), BF16),
        scratch_shapes=[pltpu.VMEM((TM_PROJ, D_MODEL), BF16)],
        compiler_params=pltpu.CompilerParams(
            dimension_semantics=("arbitrary", "arbitrary"),
            vmem_limit_bytes=VMEM_LIMIT),
        name="na_inproj",
    )(x2, g, w)


NA_BAND = NA_WIN_ROWS * GRID_W
NA_SLABS = NA_HEADS * NA_HEAD_DIM // LANES
NA_ROW_OFFS = 2 * NA_WIN_ROWS - 1
NA_COL_OFFS = 2 * NA_WIN_COLS - 1
NA_PAIR_OFFS = NA_ROW_OFFS - 1
NA_ROWS_PER_STEP = 4


def _na_build_bias(rpb_ref, bias_scr):
    c = lax.broadcasted_iota(jnp.int32, (GRID_W, LANES), 0)
    lane = lax.broadcasted_iota(jnp.int32, (GRID_W, LANES), 1)
    j = lane % GRID_W
    cs = jnp.clip(c - NA_WIN_COLS // 2, 0, GRID_W - NA_WIN_COLS)
    valid = (j >= cs) & (j < cs + NA_WIN_COLS)
    diag = jnp.where(valid, j - c + (NA_WIN_COLS - 1), -1)
    second = lax.broadcasted_iota(jnp.int32, (1, LANES), 1) >= GRID_W

    def tile(idx, carry):
        h = idx // NA_PAIR_OFFS
        o = idx % NA_PAIR_OFFS
        base = (h * NA_ROW_OFFS + o) * NA_COL_OFFS
        acc = jnp.full((GRID_W, LANES), NEG_BIG, F32)
        for k in range(NA_COL_OFFS):
            val = jnp.where(second, rpb_ref[base + NA_COL_OFFS + k], rpb_ref[base + k]) * LOG2E
            acc = jnp.where(diag == k, val, acc)
        bias_scr[h, o] = acc
        return carry

    lax.fori_loop(0, NA_HEADS * NA_PAIR_OFFS, tile, 0)


def _na_attn_kernel(zero_ref, rpb_ref, q_ref, k_ref, v_ref, o_ref, bias_scr, s0_scr, s1_scr, p0_scr, p1_scr):
    first_row = pl.program_id(1) * NA_ROWS_PER_STEP

    @pl.when((pl.program_id(0) == 0) & (first_row == 0))
    def _():
        _na_build_bias(rpb_ref, bias_scr)

    z = zero_ref[0]
    s_bufs = (s0_scr, s1_scr)
    p_bufs = (p0_scr, p1_scr)
    rows = k_ref.shape[0] // GRID_W
    first = lax.broadcasted_iota(jnp.int32, (1, LANES), 1) < NA_HEAD_DIM
    starts, offs = [], []
    for rr in range(NA_ROWS_PER_STEP):
        r = first_row + rr
        rs = jnp.clip(r - NA_WIN_ROWS // 2, 0, rows - NA_WIN_ROWS)
        starts.append(pl.multiple_of(rs * GRID_W, GRID_W))
        offs.append((NA_WIN_ROWS - 1) - (r - rs))
    work = [(rr, slab) for rr in range(NA_ROWS_PER_STEP) for slab in range(NA_SLABS)]

    def scores(i):
        rr, slab = work[i]
        cols = slice(slab * LANES, (slab + 1) * LANES)
        qs = q_ref[rr * GRID_W:(rr + 1) * GRID_W, cols]
        zero = jnp.zeros_like(qs)
        q2 = jnp.concatenate([jnp.where(first, qs, zero), jnp.where(first, zero, qs)], axis=0)
        s_bufs[i % 2][z] = _dot_nt(q2, k_ref[pl.ds(starts[rr], NA_BAND), cols])

    def softmax(i):
        rr, slab = work[i]
        bias = jnp.concatenate(
            [jnp.concatenate([bias_scr[2 * slab + e, offs[rr] + 2 * a] for a in range(NA_WIN_ROWS // 2)], axis=1)
             for e in range(2)], axis=0)
        s = s_bufs[i % 2][z] + bias
        m = jnp.max(s, axis=-1, keepdims=True)
        p_bufs[i % 2][z] = jnp.exp2(s - m).astype(BF16)

    ones = jnp.ones((NA_BAND, LANES), BF16)

    def values(i):
        rr, slab = work[i]
        cols = slice(slab * LANES, (slab + 1) * LANES)
        v1 = jnp.concatenate([v_ref[pl.ds(starts[rr], NA_BAND), cols], ones], axis=1)
        pv = _dot(p_bufs[i % 2][z], v1)
        pv = pv[:, :LANES] / pv[:, LANES:]
        o_ref[rr * GRID_W:(rr + 1) * GRID_W, cols] = jnp.where(first, pv[:GRID_W], pv[GRID_W:]).astype(BF16)

    n = len(work)
    scores(0)
    scores(1)
    softmax(0)
    for i in range(n):
        if i + 2 < n:
            scores(i + 2)
        if i + 1 < n:
            softmax(i + 1)
        values(i)


def _na_attn(qkvz, rpb_flat, batch, seq):
    hd = NA_HEADS * NA_HEAD_DIM
    steps = seq // GRID_W // NA_ROWS_PER_STEP
    tq = NA_ROWS_PER_STEP * GRID_W
    return pl.pallas_call(
        _na_attn_kernel,
        grid=(batch, steps),
        in_specs=[
            pl.BlockSpec(memory_space=pltpu.SMEM),
            pl.BlockSpec(memory_space=pltpu.SMEM),
            pl.BlockSpec((tq, hd), lambda b, r: (b * steps + r, 0)),
            pl.BlockSpec((seq, hd), lambda b, r: (b, 1)),
            pl.BlockSpec((seq, hd), lambda b, r: (b, 2)),
        ],
        out_specs=pl.BlockSpec((tq, hd), lambda b, r: (b * steps + r, 0)),
        out_shape=jax.ShapeDtypeStruct((batch * seq, hd), BF16),
        scratch_shapes=[pltpu.VMEM((NA_HEADS, NA_PAIR_OFFS, GRID_W, LANES), F32),
                        pltpu.VMEM((2, 2 * GRID_W, NA_BAND), F32),
                        pltpu.VMEM((2, 2 * GRID_W, NA_BAND), F32),
                        pltpu.VMEM((2, 2 * GRID_W, NA_BAND), BF16),
                        pltpu.VMEM((2, 2 * GRID_W, NA_BAND), BF16)],
        compiler_params=pltpu.CompilerParams(
            dimension_semantics=("arbitrary", "arbitrary"),
            vmem_limit_bytes=VMEM_LIMIT),
        name="na_attn",
    )(jnp.zeros((1,), jnp.int32), rpb_flat, qkvz, qkvz, qkvz)


def _post_kernel(x_ref, o_ref, z_ref, wo_ref, gp_ref, wg_ref, p_ref, wp_ref, gf_ref,
                 out_ref, *, feature_major, final_norm):
    if feature_major:
        z = z_ref[0].astype(F32)
        gated = (o_ref[0].astype(F32) * (z * _sigmoid(z))).astype(BF16)
        y = _dot_tn(gated, wo_ref[...])
    else:
        z = z_ref[...].astype(F32)
        gated = (o_ref[...].astype(F32) * (z * _sigmoid(z))).astype(BF16)
        y = _dot(gated, wo_ref[...])
    h = x_ref[...] + y
    hn = _rms(h, gp_ref[...]).astype(BF16)
    gate = _sigmoid(_dot(hn, wg_ref[...]))
    emb = _dot(p_ref[0].astype(BF16), wp_ref[...])
    xo = h + gate * emb
    if final_norm:
        xo = _rms(xo, gf_ref[...])
    out_ref[...] = xo


def _post_block(x2, o, z, z_col, w_out, g_ple, w_gate, p3, layer, w_proj, g_final, *,
                feature_major, final_norm, seq):
    m = x2.shape[0]
    tm = TM_PROJ
    per_seq = seq // tm
    if feature_major:
        oz_block = (1, D_MODEL, tm)
        o_spec = pl.BlockSpec(oz_block, lambda i: (i // per_seq, 0, i % per_seq))
        z_spec = pl.BlockSpec(oz_block, lambda i: (i // per_seq, 0, i % per_seq))
    else:
        o_spec = pl.BlockSpec((tm, D_MODEL), lambda i: (i, 0))
        z_spec = pl.BlockSpec((tm, D_MODEL), lambda i: (i, z_col))
    full = lambda shape: pl.BlockSpec(shape, lambda i: (0,) * len(shape))
    kern = functools.partial(_post_kernel, feature_major=feature_major, final_norm=final_norm)
    return pl.pallas_call(
        kern,
        grid=(m // tm,),
        in_specs=[
            pl.BlockSpec((tm, D_MODEL), lambda i: (i, 0)),
            o_spec,
            z_spec,
            full((D_MODEL, D_MODEL)),
            full((1, D_MODEL)),
            full((D_MODEL, D_MODEL)),
            pl.BlockSpec((1, tm, PLE_DIM), lambda i: (layer, i, 0)),
            full((PLE_DIM, D_MODEL)),
            full((1, D_MODEL)),
        ],
        out_specs=pl.BlockSpec((tm, D_MODEL), lambda i: (i, 0)),
        out_shape=jax.ShapeDtypeStruct((m, D_MODEL), F32),
        compiler_params=pltpu.CompilerParams(
            dimension_semantics=("arbitrary",),
            vmem_limit_bytes=VMEM_LIMIT),
        name="post_block_fm" if feature_major else "post_block",
    )(x2, o, z, w_out, g_ple, w_gate, p3, w_proj, g_final)


def _mla_inproj_kernel(x_ref, g_ref, wc_ref, wzt_ref, gq_ref, wq_ref, wqr_ref,
                       gkv_ref, wk_ref, wvt_ref, cq_ref, sq_ref, ck_ref, sk_ref,
                       qn_ref, qpe_ref, kn_ref, kpe_ref, vt_ref, zt_ref):
    xn = _rms(x_ref[...], g_ref[...]).astype(BF16)
    c = _dot(xn, wc_ref[...])
    zt_ref[0] = _dot_nt(wzt_ref[...], xn).astype(BF16)

    cq = _rms(c[:, :MLA_Q_RANK], gq_ref[...]).astype(BF16)
    ckv = _rms(c[:, MLA_Q_RANK:MLA_Q_RANK + MLA_KV_RANK], gkv_ref[...]).astype(BF16)
    off = MLA_Q_RANK + MLA_KV_RANK
    kpe = c[:, off:off + LANES] * ck_ref[...] + c[:, off + LANES:off + 2 * LANES] * sk_ref[...]
    kpe_ref[...] = kpe.astype(BF16)

    qt = _dot_nt(wq_ref[...], cq)
    qrot = _dot_nt(wqr_ref[...], cq)
    nope = MLA_HEADS * MLA_NOPE
    qn_ref[0] = qt[:nope].astype(BF16)
    cos_q = jnp.tile(cq_ref[...], (MLA_HEADS, 1))
    sin_q = jnp.tile(sq_ref[...], (MLA_HEADS, 1))
    qpe_ref[0] = (qt[nope:] * cos_q + qrot * sin_q).astype(BF16)

    kn_ref[...] = _dot(ckv, wk_ref[...]).astype(BF16)
    vt = _dot_nt(wvt_ref[...], ckv).astype(BF16)
    tm = vt.shape[1]
    pad_rows = lax.broadcasted_iota(jnp.int32, (MLA_V_PAD - MLA_V, MLA_BK), 0)
    ones_row = jnp.where(pad_rows == 0, 1.0, 0.0).astype(BF16)
    for h in range(MLA_HEADS):
        for s in range(tm // MLA_BK):
            vt_ref[0, h, s, :MLA_V] = vt[h * MLA_V:(h + 1) * MLA_V, s * MLA_BK:(s + 1) * MLA_BK]
            vt_ref[0, h, s, MLA_V:] = ones_row


def _mla_inproj(x2, g, wc, wzt, gq, wq, wqr, gkv, wk, wvt, cosq, sinq, cosk, sink, batch, seq):
    m = x2.shape[0]
    tm = TM_PROJ
    per_seq = seq // tm
    sub = tm // MLA_BK
    full = lambda a: pl.BlockSpec(a.shape, lambda i: (0,) * a.ndim)
    rowblk = lambda n: pl.BlockSpec((tm, n), lambda i: (i, 0))
    tabblk = lambda n: pl.BlockSpec((tm, n), lambda i: (i % per_seq, 0))
    fmtab = lambda n: pl.BlockSpec((n, tm), lambda i: (0, i % per_seq))
    fmblk = lambda n: pl.BlockSpec((1, n, tm), lambda i: (i // per_seq, 0, i % per_seq))
    nope = MLA_HEADS * MLA_NOPE
    pe = MLA_HEADS * MLA_ROPE
    return pl.pallas_call(
        _mla_inproj_kernel,
        grid=(m // tm,),
        in_specs=[rowblk(D_MODEL), full(g), full(wc), full(wzt), full(gq), full(wq), full(wqr),
                  full(gkv), full(wk), full(wvt),
                  fmtab(MLA_ROPE), fmtab(MLA_ROPE), tabblk(LANES), tabblk(LANES)],
        out_specs=[
            fmblk(nope), fmblk(pe), rowblk(nope), rowblk(LANES),
            pl.BlockSpec((1, MLA_HEADS, sub, MLA_V_PAD, MLA_BK),
                         lambda i: (i // per_seq, 0, i % per_seq, 0, 0)),
            pl.BlockSpec((1, D_MODEL, tm), lambda i: (i // per_seq, 0, i % per_seq)),
        ],
        out_shape=[
            jax.ShapeDtypeStruct((batch, nope, seq), BF16),
            jax.ShapeDtypeStruct((batch, pe, seq), BF16),
            jax.ShapeDtypeStruct((m, nope), BF16),
            jax.ShapeDtypeStruct((m, LANES), BF16),
            jax.ShapeDtypeStruct((batch, MLA_HEADS, seq // MLA_BK, MLA_V_PAD, MLA_BK), BF16),
            jax.ShapeDtypeStruct((batch, D_MODEL, seq), BF16),
        ],
        compiler_params=pltpu.CompilerParams(
            dimension_semantics=("arbitrary",),
            vmem_limit_bytes=VMEM_LIMIT),
        name="mla_inproj",
    )(x2, g, wc, wzt, gq, wq, wqr, gkv, wk, wvt, cosq, sinq, cosk, sink)


def _mla_attn_rescaling(q_scr, kn_ref, kpe_ref, vt_ref, ot_ref):
    nkb = kn_ref.shape[0] // MLA_BK
    for qb in range(MLA_QSTEPS):
        cols = slice(qb * MLA_BQ, (qb + 1) * MLA_BQ)

        def body(kb, carry, cols=cols):
            m, acc = carry
            off = pl.multiple_of(kb * MLA_BK, MLA_BK)
            k = jnp.concatenate([kn_ref[pl.ds(off, MLA_BK), :], kpe_ref[pl.ds(off, MLA_BK), :]], axis=1)
            st = _dot(k, q_scr[:, cols])
            m_new = jnp.maximum(m, jnp.max(st, axis=0, keepdims=True))
            p = jnp.exp2(st - m_new)
            acc = jnp.exp2(m - m_new) * acc + _dot(vt_ref[0, 0, kb], p.astype(BF16))
            return m_new, acc

        init = (jnp.full((1, MLA_BQ), -jnp.inf, F32), jnp.zeros((MLA_V_PAD, MLA_BQ), F32))
        _, acc = lax.fori_loop(0, nkb, body, init)
        ot_ref[0, :, cols] = (acc[:MLA_V] / acc[MLA_V:MLA_V + 1]).astype(BF16)


def _mla_attn_kernel(zero_ref, qn_ref, qpe_ref, kn_ref, kpe_ref, vt_ref, ot_ref, q_scr, s0_scr, s1_scr):
    h = pl.program_id(1)
    row = lax.broadcasted_iota(jnp.int32, (LANES, 1), 0)
    qn = qn_ref[0]
    qp = qpe_ref[0]
    q_scr[:LANES] = jnp.where(row // MLA_NOPE == h % (LANES // MLA_NOPE), qn, jnp.zeros_like(qn))
    q_scr[LANES:] = jnp.where(row // MLA_ROPE == h % (LANES // MLA_ROPE), qp, jnp.zeros_like(qp))
    nkb = kn_ref.shape[0] // MLA_BK

    z = zero_ref[0]
    s_bufs = (s0_scr, s1_scr)
    blocks = [(qb, kb) for qb in range(MLA_QSTEPS) for kb in range(nkb)]

    def scores(i):
        qb, kb = blocks[i]
        rows = slice(kb * MLA_BK, (kb + 1) * MLA_BK)
        k = jnp.concatenate([kn_ref[rows, :], kpe_ref[rows, :]], axis=1)
        s_bufs[i % 2][z] = _dot(k, q_scr[:, qb * MLA_BQ:(qb + 1) * MLA_BQ])

    finite = None
    scores(0)
    for i, (qb, kb) in enumerate(blocks):
        if i + 1 < len(blocks):
            scores(i + 1)
        if kb == 0:
            m_ref = jnp.max(s_bufs[i % 2][z], axis=0, keepdims=True)
            acc = jnp.zeros((MLA_V_PAD, MLA_BQ), F32)
        acc = acc + _dot(vt_ref[0, 0, kb], jnp.exp2(s_bufs[i % 2][z] - m_ref).astype(BF16))
        if kb == nkb - 1:
            denom = acc[MLA_V:MLA_V + 1]
            ot_ref[0, :, qb * MLA_BQ:(qb + 1) * MLA_BQ] = (acc[:MLA_V] / denom).astype(BF16)
            ok = jnp.max(denom) < MLA_DENOM_LIMIT
            finite = ok if finite is None else jnp.logical_and(finite, ok)

    @pl.when(jnp.logical_not(finite))
    def _():
        _mla_attn_rescaling(q_scr, kn_ref, kpe_ref, vt_ref, ot_ref)


def _mla_attn(qn, qpe, kn, kpe, vt, batch, seq):
    bq = MLA_QSTEPS * MLA_BQ
    per_n = LANES // MLA_NOPE
    per_r = LANES // MLA_ROPE
    return pl.pallas_call(
        _mla_attn_kernel,
        grid=(batch, MLA_HEADS, seq // bq),
        in_specs=[
            pl.BlockSpec(memory_space=pltpu.SMEM),
            pl.BlockSpec((1, LANES, bq), lambda b, h, i: (b, h // per_n, i)),
            pl.BlockSpec((1, LANES, bq), lambda b, h, i: (b, h // per_r, i)),
            pl.BlockSpec((seq, LANES), lambda b, h, i: (b, h // per_n)),
            pl.BlockSpec((seq, LANES), lambda b, h, i: (b, 0)),
            pl.BlockSpec((1, 1, seq // MLA_BK, MLA_V_PAD, MLA_BK), lambda b, h, i: (b, h, 0, 0, 0)),
        ],
        out_specs=pl.BlockSpec((1, MLA_V, bq), lambda b, h, i: (b, h, i)),
        out_shape=jax.ShapeDtypeStruct((batch, MLA_HEADS * MLA_V, seq), BF16),
        scratch_shapes=[pltpu.VMEM((2 * LANES, bq), BF16),
                        pltpu.VMEM((2, MLA_BK, MLA_BQ), F32),
                        pltpu.VMEM((2, MLA_BK, MLA_BQ), F32)],
        compiler_params=pltpu.CompilerParams(
            dimension_semantics=("arbitrary", "arbitrary", "arbitrary"),
            vmem_limit_bytes=VMEM_LIMIT),
        name="mla_attn",
    )(jnp.zeros((1,), jnp.int32), qn, qpe, kn, kpe, vt)


def _rot_cols(w):
    half = MLA_ROPE // 2
    return jnp.concatenate([-w[..., half:], w[..., :half]], axis=-1)


def _rope_tables(seq):
    inv = 1.0 / (ROPE_BASE ** (jnp.arange(0, MLA_ROPE, 2, dtype=F32) / MLA_ROPE))
    ang = jnp.arange(seq, dtype=F32)[:, None] * inv[None, :]
    cos = jnp.concatenate([jnp.cos(ang), jnp.cos(ang)], axis=-1)
    sin = jnp.concatenate([jnp.sin(ang), jnp.sin(ang)], axis=-1)
    return cos, sin


def kernel(x, p, norm_g, na_w_in, na_rpb, na_w_out, mla_w_in, mla_q_norm, mla_w_qb,
           mla_kv_norm, mla_w_kvb, mla_w_out, ple_norm, ple_w_gate, ple_w_proj, final_norm):
    batch, seq, d = x.shape
    m = batch * seq
    x2 = x.reshape(m, d)
    p3 = p.reshape(p.shape[0], m, PLE_DIM)
    row = lambda v: v.reshape(1, -1).astype(F32)

    hd = NA_HEADS * NA_HEAD_DIM
    w_in0 = na_w_in[0]
    w_in0 = jnp.concatenate([w_in0[:, :hd] * (NA_HEAD_DIM ** -0.5 * LOG2E), w_in0[:, hd:]], axis=1)
    qkvz = _na_inproj(x2, row(norm_g[0]), w_in0.astype(BF16))
    o0 = _na_attn(qkvz, na_rpb[0].reshape(-1).astype(F32), batch, seq)
    x2 = _post_block(x2, o0, qkvz, 3, na_w_out[0].astype(BF16), row(ple_norm[0]),
                     ple_w_gate[0].astype(BF16), p3, 0,
                     ple_w_proj[0].astype(BF16), row(final_norm),
                     feature_major=False, final_norm=False, seq=seq)

    w_in1 = mla_w_in[0]
    o1, o2 = MLA_Q_RANK, MLA_Q_RANK + MLA_KV_RANK
    o3 = o2 + MLA_ROPE
    w_kr = w_in1[:, o2:o3]
    rep = LANES // MLA_ROPE
    wc = jnp.concatenate([w_in1[:, :o2], jnp.tile(w_kr, (1, rep)),
                          jnp.tile(_rot_cols(w_kr), (1, rep))], axis=1).astype(BF16)
    wzt = w_in1[:, o3:].T.astype(BF16)

    scale = (MLA_NOPE + MLA_ROPE) ** -0.5 * LOG2E
    wq3 = mla_w_qb[0].reshape(MLA_Q_RANK, MLA_HEADS, MLA_NOPE + MLA_ROPE) * scale
    wq_n = wq3[:, :, :MLA_NOPE].reshape(MLA_Q_RANK, -1)
    wq_p = wq3[:, :, MLA_NOPE:]
    wq = jnp.concatenate([wq_n, wq_p.reshape(MLA_Q_RANK, -1)], axis=1).T.astype(BF16)
    wqr = _rot_cols(wq_p).reshape(MLA_Q_RANK, -1).T.astype(BF16)

    wkv3 = mla_w_kvb[0].reshape(MLA_KV_RANK, MLA_HEADS, MLA_NOPE + MLA_V)
    wk = wkv3[:, :, :MLA_NOPE].reshape(MLA_KV_RANK, -1).astype(BF16)
    wvt = wkv3[:, :, MLA_NOPE:].reshape(MLA_KV_RANK, -1).T.astype(BF16)

    cos, sin = _rope_tables(seq)
    cosq, sinq = cos.T, sin.T
    cosk, sink = jnp.tile(cos, (1, rep)), jnp.tile(sin, (1, rep))

    qn, qpe, kn, kpe, vt, zt = _mla_inproj(
        x2, row(norm_g[1]), wc, wzt, row(mla_q_norm[0]), wq, wqr, row(mla_kv_norm[0]),
        wk, wvt, cosq, sinq, cosk, sink, batch, seq)
    ot = _mla_attn(qn, qpe, kn, kpe, vt, batch, seq)
    out = _post_block(x2, ot, zt, 0, mla_w_out[0].astype(BF16), row(ple_norm[1]),
                      ple_w_gate[1].astype(BF16), p3, 1,
                      ple_w_proj[1].astype(BF16), row(final_norm),
                      feature_major=True, final_norm=True, seq=seq)
    return out.reshape(batch, seq, d)
```

```python
import functools

import jax
import jax.numpy as jnp
from jax import lax
from jax.experimental import pallas as pl
from jax.experimental.pallas import tpu as pltpu

D_MODEL = 1024
GRID_W = 64
NA_WIN_ROWS = 8
NA_WIN_COLS = 16
NA_HEADS = 16
NA_HEAD_DIM = 64
MLA_HEADS = 16
MLA_Q_RANK = 384
MLA_KV_RANK = 256
MLA_NOPE = 64
MLA_ROPE = 32
MLA_V = 64
MLA_V_PAD = 80
ROPE_BASE = 10000.0
PLE_DIM = 256
EPS = 1e-6

LANES = 128
NEG_BIG = -1e30
VMEM_LIMIT = 56 * 1024 * 1024

TM_PROJ = 512
MLA_BQ = 512
MLA_QSTEPS = 8
MLA_BK = 512
MLA_DENOM_LIMIT = 3e38
LOG2E = 1.4426950408889634

BF16 = jnp.bfloat16
F32 = jnp.float32


def _rms(x, g):
    ms = jnp.mean(x * x, axis=-1, keepdims=True)
    return x * lax.rsqrt(ms + EPS) * g


def _sigmoid(x):
    return 1.0 / (1.0 + jnp.exp(-x))


def _dot(a, b):
    return jnp.dot(a, b, preferred_element_type=F32)


def _dot_nt(a, b):
    return lax.dot_general(a, b, (((1,), (1,)), ((), ())), preferred_element_type=F32)


def _dot_tn(a, b):
    return lax.dot_general(a, b, (((0,), (0,)), ((), ())), preferred_element_type=F32)


def _na_inproj_kernel(x_ref, g_ref, w_ref, o_ref, xn_ref):
    @pl.when(pl.program_id(1) == 0)
    def _():
        xn_ref[...] = _rms(x_ref[...], g_ref[...]).astype(BF16)

    o_ref[...] = _dot(xn_ref[...], w_ref[...]).astype(BF16)


def _na_inproj(x2, g, w):
    m = x2.shape[0]
    n = w.shape[1]
    tn = n
    return pl.pallas_call(
        _na_inproj_kernel,
        grid=(m // TM_PROJ, n // tn),
        in_specs=[
            pl.BlockSpec((TM_PROJ, D_MODEL), lambda i, j: (i, 0)),
            pl.BlockSpec((1, D_MODEL), lambda i, j: (0, 0)),
            pl.BlockSpec((D_MODEL, tn), lambda i, j: (0, j)),
        ],
        out_specs=pl.BlockSpec((TM_PROJ, tn), lambda i, j: (i, j)),
        out_shape=jax.ShapeDtypeStruct((m, n), BF16),
        scratch_shapes=[pltpu.VMEM((TM_PROJ, D_MODEL), BF16)],
        compiler_params=pltpu.CompilerParams(
            dimension_semantics=("arbitrary", "arbitrary"),
            vmem_limit_bytes=VMEM_LIMIT),
        name="na_inproj",
    )(x2, g, w)


NA_BAND = NA_WIN_ROWS * GRID_W
NA_SLABS = NA_HEADS * NA_HEAD_DIM // LANES
NA_ROW_OFFS = 2 * NA_WIN_ROWS - 1
NA_COL_OFFS = 2 * NA_WIN_COLS - 1
NA_PAIR_OFFS = NA_ROW_OFFS - 1
NA_ROWS_PER_STEP = 8


def _na_build_bias(rpb_ref, bias_scr):
    c = lax.broadcasted_iota(jnp.int32, (GRID_W, LANES), 0)
    lane = lax.broadcasted_iota(jnp.int32, (GRID_W, LANES), 1)
    j = lane % GRID_W
    cs = jnp.clip(c - NA_WIN_COLS // 2, 0, GRID_W - NA_WIN_COLS)
    valid = (j >= cs) & (j < cs + NA_WIN_COLS)
    diag = jnp.where(valid, j - c + (NA_WIN_COLS - 1), -1)
    second = lax.broadcasted_iota(jnp.int32, (1, LANES), 1) >= GRID_W

    def tile(idx, carry):
        h = idx // NA_PAIR_OFFS
        o = idx % NA_PAIR_OFFS
        base = (h * NA_ROW_OFFS + o) * NA_COL_OFFS
        acc = jnp.full((GRID_W, LANES), NEG_BIG, F32)
        for k in range(NA_COL_OFFS):
            val = jnp.where(second, rpb_ref[base + NA_COL_OFFS + k], rpb_ref[base + k]) * LOG2E
            acc = jnp.where(diag == k, val, acc)
        bias_scr[h, o] = acc
        return carry

    lax.fori_loop(0, NA_HEADS * NA_PAIR_OFFS, tile, 0)


def _na_attn_kernel(zero_ref, rpb_ref, q_ref, k_ref, v_ref, o_ref, bias_scr, s0_scr, s1_scr, p0_scr, p1_scr):
    first_row = pl.program_id(1) * NA_ROWS_PER_STEP

    @pl.when((pl.program_id(0) == 0) & (first_row == 0))
    def _():
        _na_build_bias(rpb_ref, bias_scr)

    z = zero_ref[0]
    s_bufs = (s0_scr, s1_scr)
    p_bufs = (p0_scr, p1_scr)
    rows = k_ref.shape[0] // GRID_W
    first = lax.broadcasted_iota(jnp.int32, (1, LANES), 1) < NA_HEAD_DIM
    starts, offs = [], []
    for rr in range(NA_ROWS_PER_STEP):
        r = first_row + rr
        rs = jnp.clip(r - NA_WIN_ROWS // 2, 0, rows - NA_WIN_ROWS)
        starts.append(pl.multiple_of(rs * GRID_W, GRID_W))
        offs.append((NA_WIN_ROWS - 1) - (r - rs))
    work = [(rr, slab) for rr in range(NA_ROWS_PER_STEP) for slab in range(NA_SLABS)]

    def scores(i):
        rr, slab = work[i]
        cols = slice(slab * LANES, (slab + 1) * LANES)
        qs = q_ref[rr * GRID_W:(rr + 1) * GRID_W, cols]
        zero = jnp.zeros_like(qs)
        q2 = jnp.concatenate([jnp.where(first, qs, zero), jnp.where(first, zero, qs)], axis=0)
        s_bufs[i % 2][z] = _dot_nt(q2, k_ref[pl.ds(starts[rr], NA_BAND), cols])

    def softmax(i):
        rr, slab = work[i]
        bias = jnp.concatenate(
            [jnp.concatenate([bias_scr[2 * slab + e, offs[rr] + 2 * a] for a in range(NA_WIN_ROWS // 2)], axis=1)
             for e in range(2)], axis=0)
        s = s_bufs[i % 2][z] + bias
        m = jnp.max(s, axis=-1, keepdims=True)
        p_bufs[i % 2][z] = jnp.exp2(s - m).astype(BF16)

    ones = jnp.ones((NA_BAND, LANES), BF16)

    def values(i):
        rr, slab = work[i]
        cols = slice(slab * LANES, (slab + 1) * LANES)
        v1 = jnp.concatenate([v_ref[pl.ds(starts[rr], NA_BAND), cols], ones], axis=1)
        pv = _dot(p_bufs[i % 2][z], v1)
        pv = pv[:, :LANES] / pv[:, LANES:]
        o_ref[rr * GRID_W:(rr + 1) * GRID_W, cols] = jnp.where(first, pv[:GRID_W], pv[GRID_W:]).astype(BF16)

    n = len(work)
    scores(0)
    scores(1)
    softmax(0)
    for i in range(n):
        if i + 2 < n:
            scores(i + 2)
        if i + 1 < n:
            softmax(i + 1)
        values(i)


def _na_attn(qkvz, rpb_flat, batch, seq):
    hd = NA_HEADS * NA_HEAD_DIM
    steps = seq // GRID_W // NA_ROWS_PER_STEP
    tq = NA_ROWS_PER_STEP * GRID_W
    return pl.pallas_call(
        _na_attn_kernel,
        grid=(batch, steps),
        in_specs=[
            pl.BlockSpec(memory_space=pltpu.SMEM),
            pl.BlockSpec(memory_space=pltpu.SMEM),
            pl.BlockSpec((tq, hd), lambda b, r: (b * steps + r, 0)),
            pl.BlockSpec((seq, hd), lambda b, r: (b, 1)),
            pl.BlockSpec((seq, hd), lambda b, r: (b, 2)),
        ],
        out_specs=pl.BlockSpec((tq, hd), lambda b, r: (b * steps + r, 0)),
        out_shape=jax.ShapeDtypeStruct((batch * seq, hd), BF16),
        scratch_shapes=[pltpu.VMEM((NA_HEADS, NA_PAIR_OFFS, GRID_W, LANES), F32),
                        pltpu.VMEM((2, 2 * GRID_W, NA_BAND), F32),
                        pltpu.VMEM((2, 2 * GRID_W, NA_BAND), F32),
                        pltpu.VMEM((2, 2 * GRID_W, NA_BAND), BF16),
                        pltpu.VMEM((2, 2 * GRID_W, NA_BAND), BF16)],
        compiler_params=pltpu.CompilerParams(
            dimension_semantics=("arbitrary", "arbitrary"),
            vmem_limit_bytes=VMEM_LIMIT),
        name="na_attn",
    )(jnp.zeros((1,), jnp.int32), rpb_flat, qkvz, qkvz, qkvz)


def _post_kernel(x_ref, o_ref, z_ref, wo_ref, gp_ref, wg_ref, p_ref, wp_ref, gf_ref,
                 out_ref, *, feature_major, final_norm):
    if feature_major:
        z = z_ref[0].astype(F32)
        gated = (o_ref[0].astype(F32) * (z * _sigmoid(z))).astype(BF16)
        y = _dot_tn(gated, wo_ref[...])
    else:
        z = z_ref[...].astype(F32)
        gated = (o_ref[...].astype(F32) * (z * _sigmoid(z))).astype(BF16)
        y = _dot(gated, wo_ref[...])
    h = x_ref[...] + y
    hn = _rms(h, gp_ref[...]).astype(BF16)
    gate = _sigmoid(_dot(hn, wg_ref[...]))
    emb = _dot(p_ref[0].astype(BF16), wp_ref[...])
    xo = h + gate * emb
    if final_norm:
        xo = _rms(xo, gf_ref[...])
    out_ref[...] = xo


def _post_block(x2, o, z, z_col, w_out, g_ple, w_gate, p3, layer, w_proj, g_final, *,
                feature_major, final_norm, seq):
    m = x2.shape[0]
    tm = TM_PROJ
    per_seq = seq // tm
    if feature_major:
        oz_block = (1, D_MODEL, tm)
        o_spec = pl.BlockSpec(oz_block, lambda i: (i // per_seq, 0, i % per_seq))
        z_spec = pl.BlockSpec(oz_block, lambda i: (i // per_seq, 0, i % per_seq))
    else:
        o_spec = pl.BlockSpec((tm, D_MODEL), lambda i: (i, 0))
        z_spec = pl.BlockSpec((tm, D_MODEL), lambda i: (i, z_col))
    full = lambda shape: pl.BlockSpec(shape, lambda i: (0,) * len(shape))
    kern = functools.partial(_post_kernel, feature_major=feature_major, final_norm=final_norm)
    return pl.pallas_call(
        kern,
        grid=(m // tm,),
        in_specs=[
            pl.BlockSpec((tm, D_MODEL), lambda i: (i, 0)),
            o_spec,
            z_spec,
            full((D_MODEL, D_MODEL)),
            full((1, D_MODEL)),
            full((D_MODEL, D_MODEL)),
            pl.BlockSpec((1, tm, PLE_DIM), lambda i: (layer, i, 0)),
            full((PLE_DIM, D_MODEL)),
            full((1, D_MODEL)),
        ],
        out_specs=pl.BlockSpec((tm, D_MODEL), lambda i: (i, 0)),
        out_shape=jax.ShapeDtypeStruct((m, D_MODEL), F32),
        compiler_params=pltpu.CompilerParams(
            dimension_semantics=("arbitrary",),
            vmem_limit_bytes=VMEM_LIMIT),
        name="post_block_fm" if feature_major else "post_block",
    )(x2, o, z, w_out, g_ple, w_gate, p3, w_proj, g_final)


def _mla_inproj_kernel(x_ref, g_ref, wc_ref, wzt_ref, gq_ref, wq_ref, wqr_ref,
                       gkv_ref, wk_ref, wvt_ref, cq_ref, sq_ref, ck_ref, sk_ref,
                       qn_ref, qpe_ref, kn_ref, kpe_ref, vt_ref, zt_ref):
    xn = _rms(x_ref[...], g_ref[...]).astype(BF16)
    c = _dot(xn, wc_ref[...])
    zt_ref[0] = _dot_nt(wzt_ref[...], xn).astype(BF16)

    cq = _rms(c[:, :MLA_Q_RANK], gq_ref[...]).astype(BF16)
    ckv = _rms(c[:, MLA_Q_RANK:MLA_Q_RANK + MLA_KV_RANK], gkv_ref[...]).astype(BF16)
    off = MLA_Q_RANK + MLA_KV_RANK
    kpe = c[:, off:off + LANES] * ck_ref[...] + c[:, off + LANES:off + 2 * LANES] * sk_ref[...]
    kpe_ref[...] = kpe.astype(BF16)

    qt = _dot_nt(wq_ref[...], cq)
    qrot = _dot_nt(wqr_ref[...], cq)
    nope = MLA_HEADS * MLA_NOPE
    qn_ref[0] = qt[:nope].astype(BF16)
    cos_q = jnp.tile(cq_ref[...], (MLA_HEADS, 1))
    sin_q = jnp.tile(sq_ref[...], (MLA_HEADS, 1))
    qpe_ref[0] = (qt[nope:] * cos_q + qrot * sin_q).astype(BF16)

    kn_ref[...] = _dot(ckv, wk_ref[...]).astype(BF16)
    vt = _dot_nt(wvt_ref[...], ckv).astype(BF16)
    tm = vt.shape[1]
    pad_rows = lax.broadcasted_iota(jnp.int32, (MLA_V_PAD - MLA_V, MLA_BK), 0)
    ones_row = jnp.where(pad_rows == 0, 1.0, 0.0).astype(BF16)
    for h in range(MLA_HEADS):
        for s in range(tm // MLA_BK):
            vt_ref[0, h, s, :MLA_V] = vt[h * MLA_V:(h + 1) * MLA_V, s * MLA_BK:(s + 1) * MLA_BK]
            vt_ref[0, h, s, MLA_V:] = ones_row


def _mla_inproj(x2, g, wc, wzt, gq, wq, wqr, gkv, wk, wvt, cosq, sinq, cosk, sink, batch, seq):
    m = x2.shape[0]
    tm = TM_PROJ
    per_seq = seq // tm
    sub = tm // MLA_BK
    full = lambda a: pl.BlockSpec(a.shape, lambda i: (0,) * a.ndim)
    rowblk = lambda n: pl.BlockSpec((tm, n), lambda i: (i, 0))
    tabblk = lambda n: pl.BlockSpec((tm, n), lambda i: (i % per_seq, 0))
    fmtab = lambda n: pl.BlockSpec((n, tm), lambda i: (0, i % per_seq))
    fmblk = lambda n: pl.BlockSpec((1, n, tm), lambda i: (i // per_seq, 0, i % per_seq))
    nope = MLA_HEADS * MLA_NOPE
    pe = MLA_HEADS * MLA_ROPE
    return pl.pallas_call(
        _mla_inproj_kernel,
        grid=(m // tm,),
        in_specs=[rowblk(D_MODEL), full(g), full(wc), full(wzt), full(gq), full(wq), full(wqr),
                  full(gkv), full(wk), full(wvt),
                  fmtab(MLA_ROPE), fmtab(MLA_ROPE), tabblk(LANES), tabblk(LANES)],
        out_specs=[
            fmblk(nope), fmblk(pe), rowblk(nope), rowblk(LANES),
            pl.BlockSpec((1, MLA_HEADS, sub, MLA_V_PAD, MLA_BK),
                         lambda i: (i // per_seq, 0, i % per_seq, 0, 0)),
            pl.BlockSpec((1, D_MODEL, tm), lambda i: (i // per_seq, 0, i % per_seq)),
        ],
        out_shape=[
            jax.ShapeDtypeStruct((batch, nope, seq), BF16),
            jax.ShapeDtypeStruct((batch, pe, seq), BF16),
            jax.ShapeDtypeStruct((m, nope), BF16),
            jax.ShapeDtypeStruct((m, LANES), BF16),
            jax.ShapeDtypeStruct((batch, MLA_HEADS, seq // MLA_BK, MLA_V_PAD, MLA_BK), BF16),
            jax.ShapeDtypeStruct((batch, D_MODEL, seq), BF16),
        ],
        compiler_params=pltpu.CompilerParams(
            dimension_semantics=("arbitrary",),
            vmem_limit_bytes=VMEM_LIMIT),
        name="mla_inproj",
    )(x2, g, wc, wzt, gq, wq, wqr, gkv, wk, wvt, cosq, sinq, cosk, sink)


def _mla_attn_rescaling(q_scr, kn_ref, kpe_ref, vt_ref, ot_ref):
    nkb = kn_ref.shape[0] // MLA_BK
    for qb in range(MLA_QSTEPS):
        cols = slice(qb * MLA_BQ, (qb + 1) * MLA_BQ)

        def body(kb, carry, cols=cols):
            m, acc = carry
            off = pl.multiple_of(kb * MLA_BK, MLA_BK)
            k = jnp.concatenate([kn_ref[pl.ds(off, MLA_BK), :], kpe_ref[pl.ds(off, MLA_BK), :]], axis=1)
            st = _dot(k, q_scr[:, cols])
            m_new = jnp.maximum(m, jnp.max(st, axis=0, keepdims=True))
            p = jnp.exp2(st - m_new)
            acc = jnp.exp2(m - m_new) * acc + _dot(vt_ref[0, 0, kb], p.astype(BF16))
            return m_new, acc

        init = (jnp.full((1, MLA_BQ), -jnp.inf, F32), jnp.zeros((MLA_V_PAD, MLA_BQ), F32))
        _, acc = lax.fori_loop(0, nkb, body, init)
        ot_ref[0, :, cols] = (acc[:MLA_V] / acc[MLA_V:MLA_V + 1]).astype(BF16)


def _mla_attn_kernel(zero_ref, qn_ref, qpe_ref, kn_ref, kpe_ref, vt_ref, ot_ref, q_scr, s0_scr, s1_scr):
    h = pl.program_id(1)
    row = lax.broadcasted_iota(jnp.int32, (LANES, 1), 0)
    qn = qn_ref[0]
    qp = qpe_ref[0]
    q_scr[:LANES] = jnp.where(row // MLA_NOPE == h % (LANES // MLA_NOPE), qn, jnp.zeros_like(qn))
    q_scr[LANES:] = jnp.where(row // MLA_ROPE == h % (LANES // MLA_ROPE), qp, jnp.zeros_like(qp))
    nkb = kn_ref.shape[0] // MLA_BK

    z = zero_ref[0]
    s_bufs = (s0_scr, s1_scr)
    blocks = [(qb, kb) for qb in range(MLA_QSTEPS) for kb in range(nkb)]

    def scores(i):
        qb, kb = blocks[i]
        rows = slice(kb * MLA_BK, (kb + 1) * MLA_BK)
        k = jnp.concatenate([kn_ref[rows, :], kpe_ref[rows, :]], axis=1)
        s_bufs[i % 2][z] = _dot(k, q_scr[:, qb * MLA_BQ:(qb + 1) * MLA_BQ])

    finite = None
    scores(0)
    for i, (qb, kb) in enumerate(blocks):
        if i + 1 < len(blocks):
            scores(i + 1)
        if kb == 0:
            m_ref = jnp.max(s_bufs[i % 2][z], axis=0, keepdims=True)
            acc = jnp.zeros((MLA_V_PAD, MLA_BQ), F32)
        acc = acc + _dot(vt_ref[0, 0, kb], jnp.exp2(s_bufs[i % 2][z] - m_ref).astype(BF16))
        if kb == nkb - 1:
            denom = acc[MLA_V:MLA_V + 1]
            ot_ref[0, :, qb * MLA_BQ:(qb + 1) * MLA_BQ] = (acc[:MLA_V] / denom).astype(BF16)
            ok = jnp.max(denom) < MLA_DENOM_LIMIT
            finite = ok if finite is None else jnp.logical_and(finite, ok)

    @pl.when(jnp.logical_not(finite))
    def _():
        _mla_attn_rescaling(q_scr, kn_ref, kpe_ref, vt_ref, ot_ref)


def _mla_attn(qn, qpe, kn, kpe, vt, batch, seq):
    bq = MLA_QSTEPS * MLA_BQ
    per_n = LANES // MLA_NOPE
    per_r = LANES // MLA_ROPE
    return pl.pallas_call(
        _mla_attn_kernel,
        grid=(batch, MLA_HEADS, seq // bq),
        in_specs=[
            pl.BlockSpec(memory_space=pltpu.SMEM),
            pl.BlockSpec((1, LANES, bq), lambda b, h, i: (b, h // per_n, i)),
            pl.BlockSpec((1, LANES, bq), lambda b, h, i: (b, h // per_r, i)),
            pl.BlockSpec((seq, LANES), lambda b, h, i: (b, h // per_n)),
            pl.BlockSpec((seq, LANES), lambda b, h, i: (b, 0)),
            pl.BlockSpec((1, 1, seq // MLA_BK, MLA_V_PAD, MLA_BK), lambda b, h, i: (b, h, 0, 0, 0)),
        ],
        out_specs=pl.BlockSpec((1, MLA_V, bq), lambda b, h, i: (b, h, i)),
        out_shape=jax.ShapeDtypeStruct((batch, MLA_HEADS * MLA_V, seq), BF16),
        scratch_shapes=[pltpu.VMEM((2 * LANES, bq), BF16),
                        pltpu.VMEM((2, MLA_BK, MLA_BQ), F32),
                        pltpu.VMEM((2, MLA_BK, MLA_BQ), F32)],
        compiler_params=pltpu.CompilerParams(
            dimension_semantics=("arbitrary", "arbitrary", "arbitrary"),
            vmem_limit_bytes=VMEM_LIMIT),
        name="mla_attn",
    )(jnp.zeros((1,), jnp.int32), qn, qpe, kn, kpe, vt)


def _rot_cols(w):
    half = MLA_ROPE // 2
    return jnp.concatenate([-w[..., half:], w[..., :half]], axis=-1)


def _rope_tables(seq):
    inv = 1.0 / (ROPE_BASE ** (jnp.arange(0, MLA_ROPE, 2, dtype=F32) / MLA_ROPE))
    ang = jnp.arange(seq, dtype=F32)[:, None] * inv[None, :]
    cos = jnp.concatenate([jnp.cos(ang), jnp.cos(ang)], axis=-1)
    sin = jnp.concatenate([jnp.sin(ang), jnp.sin(ang)], axis=-1)
    return cos, sin


def kernel(x, p, norm_g, na_w_in, na_rpb, na_w_out, mla_w_in, mla_q_norm, mla_w_qb,
           mla_kv_norm, mla_w_kvb, mla_w_out, ple_norm, ple_w_gate, ple_w_proj, final_norm):
    batch, seq, d = x.shape
    m = batch * seq
    x2 = x.reshape(m, d)
    p3 = p.reshape(p.shape[0], m, PLE_DIM)
    row = lambda v: v.reshape(1, -1).astype(F32)

    hd = NA_HEADS * NA_HEAD_DIM
    w_in0 = na_w_in[0]
    w_in0 = jnp.concatenate([w_in0[:, :hd] * (NA_HEAD_DIM ** -0.5 * LOG2E), w_in0[:, hd:]], axis=1)
    qkvz = _na_inproj(x2, row(norm_g[0]), w_in0.astype(BF16))
    o0 = _na_attn(qkvz, na_rpb[0].reshape(-1).astype(F32), batch, seq)
    x2 = _post_block(x2, o0, qkvz, 3, na_w_out[0].astype(BF16), row(ple_norm[0]),
                     ple_w_gate[0].astype(BF16), p3, 0,
                     ple_w_proj[0].astype(BF16), row(final_norm),
                     feature_major=False, final_norm=False, seq=seq)

    w_in1 = mla_w_in[0]
    o1, o2 = MLA_Q_RANK, MLA_Q_RANK + MLA_KV_RANK
    o3 = o2 + MLA_ROPE
    w_kr = w_in1[:, o2:o3]
    rep = LANES // MLA_ROPE
    wc = jnp.concatenate([w_in1[:, :o2], jnp.tile(w_kr, (1, rep)),
                          jnp.tile(_rot_cols(w_kr), (1, rep))], axis=1).astype(BF16)
    wzt = w_in1[:, o3:].T.astype(BF16)

    scale = (MLA_NOPE + MLA_ROPE) ** -0.5 * LOG2E
    wq3 = mla_w_qb[0].reshape(MLA_Q_RANK, MLA_HEADS, MLA_NOPE + MLA_ROPE) * scale
    wq_n = wq3[:, :, :MLA_NOPE].reshape(MLA_Q_RANK, -1)
    wq_p = wq3[:, :, MLA_NOPE:]
    wq = jnp.concatenate([wq_n, wq_p.reshape(MLA_Q_RANK, -1)], axis=1).T.astype(BF16)
    wqr = _rot_cols(wq_p).reshape(MLA_Q_RANK, -1).T.astype(BF16)

    wkv3 = mla_w_kvb[0].reshape(MLA_KV_RANK, MLA_HEADS, MLA_NOPE + MLA_V)
    wk = wkv3[:, :, :MLA_NOPE].reshape(MLA_KV_RANK, -1).astype(BF16)
    wvt = wkv3[:, :, MLA_NOPE:].reshape(MLA_KV_RANK, -1).T.astype(BF16)

    cos, sin = _rope_tables(seq)
    cosq, sinq = cos.T, sin.T
    cosk, sink = jnp.tile(cos, (1, rep)), jnp.tile(sin, (1, rep))

    qn, qpe, kn, kpe, vt, zt = _mla_inproj(
        x2, row(norm_g[1]), wc, wzt, row(mla_q_norm[0]), wq, wqr, row(mla_kv_norm[0]),
        wk, wvt, cosq, sinq, cosk, sink, batch, seq)
    ot = _mla_attn(qn, qpe, kn, kpe, vt, batch, seq)
    out = _post_block(x2, ot, zt, 0, mla_w_out[0].astype(BF16), row(ple_norm[1]),
                      ple_w_gate[1].astype(BF16), p3, 1,
                      ple_w_proj[1].astype(BF16), row(final_norm),
                      feature_major=True, final_norm=True, seq=seq)
    return out.reshape(batch, seq, d)
```

```python
import functools

import jax
import jax.numpy as jnp
from jax import lax
from jax.experimental import pallas as pl
from jax.experimental.pallas import tpu as pltpu

D_MODEL = 1024
GRID_W = 64
NA_WIN_ROWS = 8
NA_WIN_COLS = 16
NA_HEADS = 16
NA_HEAD_DIM = 64
MLA_HEADS = 16
MLA_Q_RANK = 384
MLA_KV_RANK = 256
MLA_NOPE = 64
MLA_ROPE = 32
MLA_V = 64
MLA_V_PAD = 80
ROPE_BASE = 10000.0
PLE_DIM = 256
EPS = 1e-6

LANES = 128
NEG_BIG = -1e30
VMEM_LIMIT = 56 * 1024 * 1024

TM_PROJ = 1024
MLA_BQ = 512
MLA_QSTEPS = 8
MLA_BK = 512
MLA_DENOM_LIMIT = 3e38
LOG2E = 1.4426950408889634

BF16 = jnp.bfloat16
F32 = jnp.float32


def _rms(x, g):
    ms = jnp.mean(x * x, axis=-1, keepdims=True)
    return x * lax.rsqrt(ms + EPS) * g


def _sigmoid(x):
    return 1.0 / (1.0 + jnp.exp(-x))


def _dot(a, b):
    return jnp.dot(a, b, preferred_element_type=F32)


def _dot_nt(a, b):
    return lax.dot_general(a, b, (((1,), (1,)), ((), ())), preferred_element_type=F32)


def _dot_tn(a, b):
    return lax.dot_general(a, b, (((0,), (0,)), ((), ())), preferred_element_type=F32)


def _na_inproj_kernel(x_ref, g_ref, w_ref, o_ref, xn_ref):
    @pl.when(pl.program_id(1) == 0)
    def _():
        xn_ref[...] = _rms(x_ref[...], g_ref[...]).astype(BF16)

    o_ref[...] = _dot(xn_ref[...], w_ref[...]).astype(BF16)


def _na_inproj(x2, g, w):
    m = x2.shape[0]
    n = w.shape[1]
    tn = n
    return pl.pallas_call(
        _na_inproj_kernel,
        grid=(m // TM_PROJ, n // tn),
        in_specs=[
            pl.BlockSpec((TM_PROJ, D_MODEL), lambda i, j: (i, 0)),
            pl.BlockSpec((1, D_MODEL), lambda i, j: (0, 0)),
            pl.BlockSpec((D_MODEL, tn), lambda i, j: (0, j)),
        ],
        out_specs=pl.BlockSpec((TM_PROJ, tn), lambda i, j: (i, j)),
        out_shape=jax.ShapeDtypeStruct((m, n), BF16),
        scratch_shapes=[pltpu.VMEM((TM_PROJ, D_MODEL), BF16)],
        compiler_params=pltpu.CompilerParams(
            dimension_semantics=("arbitrary", "arbitrary"),
            vmem_limit_bytes=VMEM_LIMIT),
        name="na_inproj",
    )(x2, g, w)


NA_BAND = NA_WIN_ROWS * GRID_W
NA_SLABS = NA_HEADS * NA_HEAD_DIM // LANES
NA_ROW_OFFS = 2 * NA_WIN_ROWS - 1
NA_COL_OFFS = 2 * NA_WIN_COLS - 1
NA_PAIR_OFFS = NA_ROW_OFFS - 1
NA_ROWS_PER_STEP = 8


def _na_build_bias(rpb_ref, bias_scr):
    c = lax.broadcasted_iota(jnp.int32, (GRID_W, LANES), 0)
    lane = lax.broadcasted_iota(jnp.int32, (GRID_W, LANES), 1)
    j = lane % GRID_W
    cs = jnp.clip(c - NA_WIN_COLS // 2, 0, GRID_W - NA_WIN_COLS)
    valid = (j >= cs) & (j < cs + NA_WIN_COLS)
    diag = jnp.where(valid, j - c + (NA_WIN_COLS - 1), -1)
    second = lax.broadcasted_iota(jnp.int32, (1, LANES), 1) >= GRID_W

    def tile(idx, carry):
        h = idx // NA_PAIR_OFFS
        o = idx % NA_PAIR_OFFS
        base = (h * NA_ROW_OFFS + o) * NA_COL_OFFS
        acc = jnp.full((GRID_W, LANES), NEG_BIG, F32)
        for k in range(NA_COL_OFFS):
            val = jnp.where(second, rpb_ref[base + NA_COL_OFFS + k], rpb_ref[base + k]) * LOG2E
            acc = jnp.where(diag == k, val, acc)
        bias_scr[h, o] = acc
        return carry

    lax.fori_loop(0, NA_HEADS * NA_PAIR_OFFS, tile, 0)


def _na_attn_kernel(zero_ref, rpb_ref, q_ref, k_ref, v_ref, o_ref, bias_scr, s0_scr, s1_scr, p0_scr, p1_scr):
    first_row = pl.program_id(1) * NA_ROWS_PER_STEP

    @pl.when((pl.program_id(0) == 0) & (first_row == 0))
    def _():
        _na_build_bias(rpb_ref, bias_scr)

    z = zero_ref[0]
    s_bufs = (s0_scr, s1_scr)
    p_bufs = (p0_scr, p1_scr)
    rows = k_ref.shape[0] // GRID_W
    first = lax.broadcasted_iota(jnp.int32, (1, LANES), 1) < NA_HEAD_DIM
    starts, offs = [], []
    for rr in range(NA_ROWS_PER_STEP):
        r = first_row + rr
        rs = jnp.clip(r - NA_WIN_ROWS // 2, 0, rows - NA_WIN_ROWS)
        starts.append(pl.multiple_of(rs * GRID_W, GRID_W))
        offs.append((NA_WIN_ROWS - 1) - (r - rs))
    work = [(rr, slab) for rr in range(NA_ROWS_PER_STEP) for slab in range(NA_SLABS)]

    def scores(i):
        rr, slab = work[i]
        cols = slice(slab * LANES, (slab + 1) * LANES)
        qs = q_ref[rr * GRID_W:(rr + 1) * GRID_W, cols]
        zero = jnp.zeros_like(qs)
        q2 = jnp.concatenate([jnp.where(first, qs, zero), jnp.where(first, zero, qs)], axis=0)
        s_bufs[i % 2][z] = _dot_nt(q2, k_ref[pl.ds(starts[rr], NA_BAND), cols])

    def softmax(i):
        rr, slab = work[i]
        bias = jnp.concatenate(
            [jnp.concatenate([bias_scr[2 * slab + e, offs[rr] + 2 * a] for a in range(NA_WIN_ROWS // 2)], axis=1)
             for e in range(2)], axis=0)
        s = s_bufs[i % 2][z] + bias
        m = jnp.max(s, axis=-1, keepdims=True)
        p_bufs[i % 2][z] = jnp.exp2(s - m).astype(BF16)

    ones = jnp.ones((NA_BAND, LANES), BF16)

    def values(i):
        rr, slab = work[i]
        cols = slice(slab * LANES, (slab + 1) * LANES)
        v1 = jnp.concatenate([v_ref[pl.ds(starts[rr], NA_BAND), cols], ones], axis=1)
        pv = _dot(p_bufs[i % 2][z], v1)
        pv = pv[:, :LANES] / pv[:, LANES:]
        o_ref[rr * GRID_W:(rr + 1) * GRID_W, cols] = jnp.where(first, pv[:GRID_W], pv[GRID_W:]).astype(BF16)

    n = len(work)
    scores(0)
    scores(1)
    softmax(0)
    for i in range(n):
        if i + 2 < n:
            scores(i + 2)
        if i + 1 < n:
            softmax(i + 1)
        values(i)


def _na_attn(qkvz, rpb_flat, batch, seq):
    hd = NA_HEADS * NA_HEAD_DIM
    steps = seq // GRID_W // NA_ROWS_PER_STEP
    tq = NA_ROWS_PER_STEP * GRID_W
    return pl.pallas_call(
        _na_attn_kernel,
        grid=(batch, steps),
        in_specs=[
            pl.BlockSpec(memory_space=pltpu.SMEM),
            pl.BlockSpec(memory_space=pltpu.SMEM),
            pl.BlockSpec((tq, hd), lambda b, r: (b * steps + r, 0)),
            pl.BlockSpec((seq, hd), lambda b, r: (b, 1)),
            pl.BlockSpec((seq, hd), lambda b, r: (b, 2)),
        ],
        out_specs=pl.BlockSpec((tq, hd), lambda b, r: (b * steps + r, 0)),
        out_shape=jax.ShapeDtypeStruct((batch * seq, hd), BF16),
        scratch_shapes=[pltpu.VMEM((NA_HEADS, NA_PAIR_OFFS, GRID_W, LANES), F32),
                        pltpu.VMEM((2, 2 * GRID_W, NA_BAND), F32),
                        pltpu.VMEM((2, 2 * GRID_W, NA_BAND), F32),
                        pltpu.VMEM((2, 2 * GRID_W, NA_BAND), BF16),
                        pltpu.VMEM((2, 2 * GRID_W, NA_BAND), BF16)],
        compiler_params=pltpu.CompilerParams(
            dimension_semantics=("arbitrary", "arbitrary"),
            vmem_limit_bytes=VMEM_LIMIT),
        name="na_attn",
    )(jnp.zeros((1,), jnp.int32), rpb_flat, qkvz, qkvz, qkvz)


def _post_kernel(x_ref, o_ref, z_ref, wo_ref, gp_ref, wg_ref, p_ref, wp_ref, gf_ref,
                 out_ref, *, feature_major, final_norm):
    if feature_major:
        z = z_ref[0].astype(F32)
        gated = (o_ref[0].astype(F32) * (z * _sigmoid(z))).astype(BF16)
        y = _dot_tn(gated, wo_ref[...])
    else:
        z = z_ref[...].astype(F32)
        gated = (o_ref[...].astype(F32) * (z * _sigmoid(z))).astype(BF16)
        y = _dot(gated, wo_ref[...])
    h = x_ref[...] + y
    hn = _rms(h, gp_ref[...]).astype(BF16)
    gate = _sigmoid(_dot(hn, wg_ref[...]))
    emb = _dot(p_ref[0].astype(BF16), wp_ref[...])
    xo = h + gate * emb
    if final_norm:
        xo = _rms(xo, gf_ref[...])
    out_ref[...] = xo


def _post_block(x2, o, z, z_col, w_out, g_ple, w_gate, p3, layer, w_proj, g_final, *,
                feature_major, final_norm, seq):
    m = x2.shape[0]
    tm = TM_PROJ
    per_seq = seq // tm
    if feature_major:
        oz_block = (1, D_MODEL, tm)
        o_spec = pl.BlockSpec(oz_block, lambda i: (i // per_seq, 0, i % per_seq))
        z_spec = pl.BlockSpec(oz_block, lambda i: (i // per_seq, 0, i % per_seq))
    else:
        o_spec = pl.BlockSpec((tm, D_MODEL), lambda i: (i, 0))
        z_spec = pl.BlockSpec((tm, D_MODEL), lambda i: (i, z_col))
    full = lambda shape: pl.BlockSpec(shape, lambda i: (0,) * len(shape))
    kern = functools.partial(_post_kernel, feature_major=feature_major, final_norm=final_norm)
    return pl.pallas_call(
        kern,
        grid=(m // tm,),
        in_specs=[
            pl.BlockSpec((tm, D_MODEL), lambda i: (i, 0)),
            o_spec,
            z_spec,
            full((D_MODEL, D_MODEL)),
            full((1, D_MODEL)),
            full((D_MODEL, D_MODEL)),
            pl.BlockSpec((1, tm, PLE_DIM), lambda i: (layer, i, 0)),
            full((PLE_DIM, D_MODEL)),
            full((1, D_MODEL)),
        ],
        out_specs=pl.BlockSpec((tm, D_MODEL), lambda i: (i, 0)),
        out_shape=jax.ShapeDtypeStruct((m, D_MODEL), F32),
        compiler_params=pltpu.CompilerParams(
            dimension_semantics=("arbitrary",),
            vmem_limit_bytes=VMEM_LIMIT),
        name="post_block_fm" if feature_major else "post_block",
    )(x2, o, z, w_out, g_ple, w_gate, p3, w_proj, g_final)


def _mla_inproj_kernel(x_ref, g_ref, wc_ref, wzt_ref, gq_ref, wq_ref,
                       gkv_ref, wk_ref, wvt_ref, cq_ref, sq_ref, ck_ref, sk_ref,
                       qn_ref, qpe_ref, kn_ref, kpe_ref, vt_ref, zt_ref):
    xn = _rms(x_ref[...], g_ref[...]).astype(BF16)
    c = _dot(xn, wc_ref[...])
    zt_ref[0] = _dot_nt(wzt_ref[...], xn).astype(BF16)

    cq = _rms(c[:, :MLA_Q_RANK], gq_ref[...]).astype(BF16)
    ckv = _rms(c[:, MLA_Q_RANK:MLA_Q_RANK + MLA_KV_RANK], gkv_ref[...]).astype(BF16)
    off = MLA_Q_RANK + MLA_KV_RANK
    kr = c[:, off:off + LANES]
    half = MLA_ROPE // 2
    lane = lax.broadcasted_iota(jnp.int32, (1, LANES), 1)
    kr_rot = jnp.where(lane % MLA_ROPE < half, -pltpu.roll(kr, LANES - half, 1), pltpu.roll(kr, half, 1))
    kpe_ref[...] = (kr * ck_ref[...] + kr_rot * sk_ref[...]).astype(BF16)

    qt = _dot_nt(wq_ref[...], cq)
    nope = MLA_HEADS * MLA_NOPE
    qn_ref[0] = qt[:nope].astype(BF16)
    cos_h, sin_h = cq_ref[:half], sq_ref[:half]
    for h in range(MLA_HEADS):
        x1 = qt[nope + h * MLA_ROPE:nope + h * MLA_ROPE + half]
        x2 = qt[nope + h * MLA_ROPE + half:nope + (h + 1) * MLA_ROPE]
        qpe_ref[0, h * MLA_ROPE:h * MLA_ROPE + half] = (x1 * cos_h - x2 * sin_h).astype(BF16)
        qpe_ref[0, h * MLA_ROPE + half:(h + 1) * MLA_ROPE] = (x2 * cos_h + x1 * sin_h).astype(BF16)

    kn_ref[...] = _dot(ckv, wk_ref[...]).astype(BF16)
    vt = _dot_nt(wvt_ref[...], ckv).astype(BF16)
    tm = vt.shape[1]
    pad_rows = lax.broadcasted_iota(jnp.int32, (MLA_V_PAD - MLA_V, MLA_BK), 0)
    ones_row = jnp.where(pad_rows == 0, 1.0, 0.0).astype(BF16)
    for h in range(MLA_HEADS):
        for s in range(tm // MLA_BK):
            vt_ref[0, h, s, :MLA_V] = vt[h * MLA_V:(h + 1) * MLA_V, s * MLA_BK:(s + 1) * MLA_BK]
            vt_ref[0, h, s, MLA_V:] = ones_row


def _mla_inproj(x2, g, wc, wzt, gq, wq, gkv, wk, wvt, cosq, sinq, cosk, sink, batch, seq):
    m = x2.shape[0]
    tm = TM_PROJ
    per_seq = seq // tm
    sub = tm // MLA_BK
    full = lambda a: pl.BlockSpec(a.shape, lambda i: (0,) * a.ndim)
    rowblk = lambda n: pl.BlockSpec((tm, n), lambda i: (i, 0))
    tabblk = lambda n: pl.BlockSpec((tm, n), lambda i: (i % per_seq, 0))
    fmtab = lambda n: pl.BlockSpec((n, tm), lambda i: (0, i % per_seq))
    fmblk = lambda n: pl.BlockSpec((1, n, tm), lambda i: (i // per_seq, 0, i % per_seq))
    nope = MLA_HEADS * MLA_NOPE
    pe = MLA_HEADS * MLA_ROPE
    return pl.pallas_call(
        _mla_inproj_kernel,
        grid=(m // tm,),
        in_specs=[rowblk(D_MODEL), full(g), full(wc), full(wzt), full(gq), full(wq),
                  full(gkv), full(wk), full(wvt),
                  fmtab(MLA_ROPE), fmtab(MLA_ROPE), tabblk(LANES), tabblk(LANES)],
        out_specs=[
            fmblk(nope), fmblk(pe), rowblk(nope), rowblk(LANES),
            pl.BlockSpec((1, MLA_HEADS, sub, MLA_V_PAD, MLA_BK),
                         lambda i: (i // per_seq, 0, i % per_seq, 0, 0)),
            pl.BlockSpec((1, D_MODEL, tm), lambda i: (i // per_seq, 0, i % per_seq)),
        ],
        out_shape=[
            jax.ShapeDtypeStruct((batch, nope, seq), BF16),
            jax.ShapeDtypeStruct((batch, pe, seq), BF16),
            jax.ShapeDtypeStruct((m, nope), BF16),
            jax.ShapeDtypeStruct((m, LANES), BF16),
            jax.ShapeDtypeStruct((batch, MLA_HEADS, seq // MLA_BK, MLA_V_PAD, MLA_BK), BF16),
            jax.ShapeDtypeStruct((batch, D_MODEL, seq), BF16),
        ],
        compiler_params=pltpu.CompilerParams(
            dimension_semantics=("arbitrary",),
            vmem_limit_bytes=VMEM_LIMIT),
        name="mla_inproj",
    )(x2, g, wc, wzt, gq, wq, gkv, wk, wvt, cosq, sinq, cosk, sink)


def _mla_attn_rescaling(q_scr, kn_ref, kpe_ref, vt_ref, ot_ref):
    nkb = kn_ref.shape[0] // MLA_BK
    for qb in range(MLA_QSTEPS):
        cols = slice(qb * MLA_BQ, (qb + 1) * MLA_BQ)

        def body(kb, carry, cols=cols):
            m, acc = carry
            off = pl.multiple_of(kb * MLA_BK, MLA_BK)
            k = jnp.concatenate([kn_ref[pl.ds(off, MLA_BK), :], kpe_ref[pl.ds(off, MLA_BK), :]], axis=1)
            st = _dot(k, q_scr[:, cols])
            m_new = jnp.maximum(m, jnp.max(st, axis=0, keepdims=True))
            p = jnp.exp2(st - m_new)
            acc = jnp.exp2(m - m_new) * acc + _dot(vt_ref[0, 0, kb], p.astype(BF16))
            return m_new, acc

        init = (jnp.full((1, MLA_BQ), -jnp.inf, F32), jnp.zeros((MLA_V_PAD, MLA_BQ), F32))
        _, acc = lax.fori_loop(0, nkb, body, init)
        ot_ref[0, :, cols] = (acc[:MLA_V] / acc[MLA_V:MLA_V + 1]).astype(BF16)


def _mla_attn_kernel(zero_ref, qn_ref, qpe_ref, kn_ref, kpe_ref, vt_ref, ot_ref, q_scr, s0_scr, s1_scr):
    h = pl.program_id(1)
    row = lax.broadcasted_iota(jnp.int32, (LANES, 1), 0)
    qn = qn_ref[0]
    qp = qpe_ref[0]
    q_scr[:LANES] = jnp.where(row // MLA_NOPE == h % (LANES // MLA_NOPE), qn, jnp.zeros_like(qn))
    q_scr[LANES:] = jnp.where(row // MLA_ROPE == h % (LANES // MLA_ROPE), qp, jnp.zeros_like(qp))
    nkb = kn_ref.shape[0] // MLA_BK

    z = zero_ref[0]
    s_bufs = (s0_scr, s1_scr)
    blocks = [(qb, kb) for qb in range(MLA_QSTEPS) for kb in range(nkb)]

    def scores(i):
        qb, kb = blocks[i]
        rows = slice(kb * MLA_BK, (kb + 1) * MLA_BK)
        k = jnp.concatenate([kn_ref[rows, :], kpe_ref[rows, :]], axis=1)
        s_bufs[i % 2][z] = _dot(k, q_scr[:, qb * MLA_BQ:(qb + 1) * MLA_BQ])

    finite = None
    scores(0)
    for i, (qb, kb) in enumerate(blocks):
        if i + 1 < len(blocks):
            scores(i + 1)
        if kb == 0:
            m_ref = jnp.max(s_bufs[i % 2][z], axis=0, keepdims=True)
            acc = jnp.zeros((MLA_V_PAD, MLA_BQ), F32)
        acc = acc + _dot(vt_ref[0, 0, kb], jnp.exp2(s_bufs[i % 2][z] - m_ref).astype(BF16))
        if kb == nkb - 1:
            denom = acc[MLA_V:MLA_V + 1]
            ot_ref[0, :, qb * MLA_BQ:(qb + 1) * MLA_BQ] = (acc[:MLA_V] / denom).astype(BF16)
            ok = jnp.max(denom) < MLA_DENOM_LIMIT
            finite = ok if finite is None else jnp.logical_and(finite, ok)

    @pl.when(jnp.logical_not(finite))
    def _():
        _mla_attn_rescaling(q_scr, kn_ref, kpe_ref, vt_ref, ot_ref)


def _mla_attn(qn, qpe, kn, kpe, vt, batch, seq):
    bq = MLA_QSTEPS * MLA_BQ
    per_n = LANES // MLA_NOPE
    per_r = LANES // MLA_ROPE
    return pl.pallas_call(
        _mla_attn_kernel,
        grid=(batch, MLA_HEADS, seq // bq),
        in_specs=[
            pl.BlockSpec(memory_space=pltpu.SMEM),
            pl.BlockSpec((1, LANES, bq), lambda b, h, i: (b, h // per_n, i)),
            pl.BlockSpec((1, LANES, bq), lambda b, h, i: (b, h // per_r, i)),
            pl.BlockSpec((seq, LANES), lambda b, h, i: (b, h // per_n)),
            pl.BlockSpec((seq, LANES), lambda b, h, i: (b, 0)),
            pl.BlockSpec((1, 1, seq // MLA_BK, MLA_V_PAD, MLA_BK), lambda b, h, i: (b, h, 0, 0, 0)),
        ],
        out_specs=pl.BlockSpec((1, MLA_V, bq), lambda b, h, i: (b, h, i)),
        out_shape=jax.ShapeDtypeStruct((batch, MLA_HEADS * MLA_V, seq), BF16),
        scratch_shapes=[pltpu.VMEM((2 * LANES, bq), BF16),
                        pltpu.VMEM((2, MLA_BK, MLA_BQ), F32),
                        pltpu.VMEM((2, MLA_BK, MLA_BQ), F32)],
        compiler_params=pltpu.CompilerParams(
            dimension_semantics=("arbitrary", "arbitrary", "arbitrary"),
            vmem_limit_bytes=VMEM_LIMIT),
        name="mla_attn",
    )(jnp.zeros((1,), jnp.int32), qn, qpe, kn, kpe, vt)


def _rope_tables(seq):
    inv = 1.0 / (ROPE_BASE ** (jnp.arange(0, MLA_ROPE, 2, dtype=F32) / MLA_ROPE))
    ang = jnp.arange(seq, dtype=F32)[:, None] * inv[None, :]
    cos = jnp.concatenate([jnp.cos(ang), jnp.cos(ang)], axis=-1)
    sin = jnp.concatenate([jnp.sin(ang), jnp.sin(ang)], axis=-1)
    return cos, sin


def kernel(x, p, norm_g, na_w_in, na_rpb, na_w_out, mla_w_in, mla_q_norm, mla_w_qb,
           mla_kv_norm, mla_w_kvb, mla_w_out, ple_norm, ple_w_gate, ple_w_proj, final_norm):
    batch, seq, d = x.shape
    m = batch * seq
    x2 = x.reshape(m, d)
    p3 = p.reshape(p.shape[0], m, PLE_DIM)
    row = lambda v: v.reshape(1, -1).astype(F32)

    hd = NA_HEADS * NA_HEAD_DIM
    q_scale = jnp.where(jnp.arange(4 * hd) < hd, NA_HEAD_DIM ** -0.5 * LOG2E, 1.0).astype(F32)
    w_in0 = (na_w_in.reshape(d, 4 * hd) * q_scale).astype(BF16)
    qkvz = _na_inproj(x2, row(norm_g[0]), w_in0)
    o0 = _na_attn(qkvz, na_rpb[0].reshape(-1).astype(F32), batch, seq)
    x2 = _post_block(x2, o0, qkvz, 3, na_w_out[0].astype(BF16), row(ple_norm[0]),
                     ple_w_gate[0].astype(BF16), p3, 0,
                     ple_w_proj[0].astype(BF16), row(final_norm),
                     feature_major=False, final_norm=False, seq=seq)

    w_in1 = mla_w_in[0]
    o1, o2 = MLA_Q_RANK, MLA_Q_RANK + MLA_KV_RANK
    o3 = o2 + MLA_ROPE
    w_kr = w_in1[:, o2:o3]
    rep = LANES // MLA_ROPE
    wc = jnp.concatenate([w_in1[:, :o2], jnp.tile(w_kr, (1, rep))], axis=1).astype(BF16)
    wzt = w_in1[:, o3:].T.astype(BF16)

    scale = (MLA_NOPE + MLA_ROPE) ** -0.5 * LOG2E
    wq3 = mla_w_qb[0].reshape(MLA_Q_RANK, MLA_HEADS, MLA_NOPE + MLA_ROPE) * scale
    wq_n = wq3[:, :, :MLA_NOPE].reshape(MLA_Q_RANK, -1)
    wq_p = wq3[:, :, MLA_NOPE:]
    wq = jnp.concatenate([wq_n, wq_p.reshape(MLA_Q_RANK, -1)], axis=1).T.astype(BF16)

    wkv3 = mla_w_kvb[0].reshape(MLA_KV_RANK, MLA_HEADS, MLA_NOPE + MLA_V)
    wk = wkv3[:, :, :MLA_NOPE].reshape(MLA_KV_RANK, -1).astype(BF16)
    wvt = wkv3[:, :, MLA_NOPE:].reshape(MLA_KV_RANK, -1).T.astype(BF16)

    cos, sin = _rope_tables(seq)
    cosq, sinq = cos.T, sin.T
    cosk, sink = jnp.tile(cos, (1, rep)), jnp.tile(sin, (1, rep))

    qn, qpe, kn, kpe, vt, zt = _mla_inproj(
        x2, row(norm_g[1]), wc, wzt, row(mla_q_norm[0]), wq, row(mla_kv_norm[0]),
        wk, wvt, cosq, sinq, cosk, sink, batch, seq)
    ot = _mla_attn(qn, qpe, kn, kpe, vt, batch, seq)
    out = _post_block(x2, ot, zt, 0, mla_w_out[0].astype(BF16), row(ple_norm[1]),
                      ple_w_gate[1].astype(BF16), p3, 1,
                      ple_w_proj[1].astype(BF16), row(final_norm),
                      feature_major=True, final_norm=True, seq=seq)
    return out.reshape(batch, seq, d)
```

```python
import functools

import jax
import jax.numpy as jnp
from jax import lax
from jax.experimental import pallas as pl
from jax.experimental.pallas import tpu as pltpu

D_MODEL = 1024
GRID_W = 64
NA_WIN_ROWS = 8
NA_WIN_COLS = 16
NA_HEADS = 16
NA_HEAD_DIM = 64
MLA_HEADS = 16
MLA_Q_RANK = 384
MLA_KV_RANK = 256
MLA_NOPE = 64
MLA_ROPE = 32
MLA_V = 64
MLA_V_PAD = 128
ROPE_BASE = 10000.0
PLE_DIM = 256
EPS = 1e-6

LANES = 128
NEG_BIG = -1e30
VMEM_LIMIT = 56 * 1024 * 1024

TM_PROJ = 1024
MLA_BQ = 512
MLA_QSTEPS = 8
MLA_BK = 512
MLA_DENOM_LIMIT = 3e38
LOG2E = 1.4426950408889634

BF16 = jnp.bfloat16
F32 = jnp.float32


def _rms(x, g):
    ms = jnp.mean(x * x, axis=-1, keepdims=True)
    return x * lax.rsqrt(ms + EPS) * g


def _sigmoid(x):
    return 1.0 / (1.0 + jnp.exp(-x))


def _dot(a, b):
    return jnp.dot(a, b, preferred_element_type=F32)


def _dot_nt(a, b):
    return lax.dot_general(a, b, (((1,), (1,)), ((), ())), preferred_element_type=F32)


def _dot_tn(a, b):
    return lax.dot_general(a, b, (((0,), (0,)), ((), ())), preferred_element_type=F32)


def _na_inproj_kernel(x_ref, g_ref, w_ref, o_ref, xn_ref):
    @pl.when(pl.program_id(1) == 0)
    def _():
        xn_ref[...] = _rms(x_ref[...], g_ref[...]).astype(BF16)

    o_ref[...] = _dot(xn_ref[...], w_ref[...]).astype(BF16)


def _na_inproj(x2, g, w):
    m = x2.shape[0]
    n = w.shape[1]
    tn = n
    return pl.pallas_call(
        _na_inproj_kernel,
        grid=(m // TM_PROJ, n // tn),
        in_specs=[
            pl.BlockSpec((TM_PROJ, D_MODEL), lambda i, j: (i, 0)),
            pl.BlockSpec((1, D_MODEL), lambda i, j: (0, 0)),
            pl.BlockSpec((D_MODEL, tn), lambda i, j: (0, j)),
        ],
        out_specs=pl.BlockSpec((TM_PROJ, tn), lambda i, j: (i, j)),
        out_shape=jax.ShapeDtypeStruct((m, n), BF16),
        scratch_shapes=[pltpu.VMEM((TM_PROJ, D_MODEL), BF16)],
        compiler_params=pltpu.CompilerParams(
            dimension_semantics=("arbitrary", "arbitrary"),
            vmem_limit_bytes=VMEM_LIMIT),
        name="na_inproj",
    )(x2, g, w)


NA_BAND = NA_WIN_ROWS * GRID_W
NA_SLABS = NA_HEADS * NA_HEAD_DIM // LANES
NA_ROW_OFFS = 2 * NA_WIN_ROWS - 1
NA_COL_OFFS = 2 * NA_WIN_COLS - 1
NA_PAIR_OFFS = NA_ROW_OFFS - 1
NA_ROWS_PER_STEP = 8


def _na_build_bias(rpb_ref, bias_scr):
    c = lax.broadcasted_iota(jnp.int32, (GRID_W, LANES), 0)
    lane = lax.broadcasted_iota(jnp.int32, (GRID_W, LANES), 1)
    j = lane % GRID_W
    cs = jnp.clip(c - NA_WIN_COLS // 2, 0, GRID_W - NA_WIN_COLS)
    valid = (j >= cs) & (j < cs + NA_WIN_COLS)
    diag = jnp.where(valid, j - c + (NA_WIN_COLS - 1), -1)
    second = lax.broadcasted_iota(jnp.int32, (1, LANES), 1) >= GRID_W

    def tile(idx, carry):
        h = idx // NA_PAIR_OFFS
        o = idx % NA_PAIR_OFFS
        base = (h * NA_ROW_OFFS + o) * NA_COL_OFFS
        acc = jnp.full((GRID_W, LANES), NEG_BIG, F32)
        for k in range(NA_COL_OFFS):
            val = jnp.where(second, rpb_ref[base + NA_COL_OFFS + k], rpb_ref[base + k]) * LOG2E
            acc = jnp.where(diag == k, val, acc)
        bias_scr[h, o] = acc
        return carry

    lax.fori_loop(0, NA_HEADS * NA_PAIR_OFFS, tile, 0)


def _na_attn_kernel(zero_ref, rpb_ref, q_ref, k_ref, v_ref, o_ref, bias_scr, s0_scr, s1_scr, p0_scr, p1_scr):
    first_row = pl.program_id(1) * NA_ROWS_PER_STEP

    @pl.when((pl.program_id(0) == 0) & (first_row == 0))
    def _():
        _na_build_bias(rpb_ref, bias_scr)

    z = zero_ref[0]
    s_bufs = (s0_scr, s1_scr)
    p_bufs = (p0_scr, p1_scr)
    rows = k_ref.shape[0] // GRID_W
    first = lax.broadcasted_iota(jnp.int32, (1, LANES), 1) < NA_HEAD_DIM
    starts, offs = [], []
    for rr in range(NA_ROWS_PER_STEP):
        r = first_row + rr
        rs = jnp.clip(r - NA_WIN_ROWS // 2, 0, rows - NA_WIN_ROWS)
        starts.append(pl.multiple_of(rs * GRID_W, GRID_W))
        offs.append((NA_WIN_ROWS - 1) - (r - rs))
    work = [(rr, slab) for rr in range(NA_ROWS_PER_STEP) for slab in range(NA_SLABS)]

    def scores(i):
        rr, slab = work[i]
        cols = slice(slab * LANES, (slab + 1) * LANES)
        qs = q_ref[rr * GRID_W:(rr + 1) * GRID_W, cols]
        zero = jnp.zeros_like(qs)
        q2 = jnp.concatenate([jnp.where(first, qs, zero), jnp.where(first, zero, qs)], axis=0)
        s_bufs[i % 2][z] = _dot_nt(q2, k_ref[pl.ds(starts[rr], NA_BAND), cols])

    def softmax(i):
        rr, slab = work[i]
        bias = jnp.concatenate(
            [jnp.concatenate([bias_scr[2 * slab + e, offs[rr] + 2 * a] for a in range(NA_WIN_ROWS // 2)], axis=1)
             for e in range(2)], axis=0)
        s = s_bufs[i % 2][z] + bias
        m = jnp.max(s, axis=-1, keepdims=True)
        p_bufs[i % 2][z] = jnp.exp2(s - m).astype(BF16)

    ones = jnp.ones((NA_BAND, LANES), BF16)

    def values(i):
        rr, slab = work[i]
        cols = slice(slab * LANES, (slab + 1) * LANES)
        v1 = jnp.concatenate([v_ref[pl.ds(starts[rr], NA_BAND), cols], ones], axis=1)
        pv = _dot(p_bufs[i % 2][z], v1)
        pv = pv[:, :LANES] / pv[:, LANES:]
        o_ref[rr * GRID_W:(rr + 1) * GRID_W, cols] = jnp.where(first, pv[:GRID_W], pv[GRID_W:]).astype(BF16)

    n = len(work)
    scores(0)
    scores(1)
    softmax(0)
    for i in range(n):
        if i + 2 < n:
            scores(i + 2)
        if i + 1 < n:
            softmax(i + 1)
        values(i)


def _na_attn(qkvz, rpb_flat, batch, seq):
    hd = NA_HEADS * NA_HEAD_DIM
    steps = seq // GRID_W // NA_ROWS_PER_STEP
    tq = NA_ROWS_PER_STEP * GRID_W
    return pl.pallas_call(
        _na_attn_kernel,
        grid=(batch, steps),
        in_specs=[
            pl.BlockSpec(memory_space=pltpu.SMEM),
            pl.BlockSpec(memory_space=pltpu.SMEM),
            pl.BlockSpec((tq, hd), lambda b, r: (b * steps + r, 0)),
            pl.BlockSpec((seq, hd), lambda b, r: (b, 1)),
            pl.BlockSpec((seq, hd), lambda b, r: (b, 2)),
        ],
        out_specs=pl.BlockSpec((tq, hd), lambda b, r: (b * steps + r, 0)),
        out_shape=jax.ShapeDtypeStruct((batch * seq, hd), BF16),
        scratch_shapes=[pltpu.VMEM((NA_HEADS, NA_PAIR_OFFS, GRID_W, LANES), F32),
                        pltpu.VMEM((2, 2 * GRID_W, NA_BAND), F32),
                        pltpu.VMEM((2, 2 * GRID_W, NA_BAND), F32),
                        pltpu.VMEM((2, 2 * GRID_W, NA_BAND), BF16),
                        pltpu.VMEM((2, 2 * GRID_W, NA_BAND), BF16)],
        compiler_params=pltpu.CompilerParams(
            dimension_semantics=("arbitrary", "arbitrary"),
            vmem_limit_bytes=VMEM_LIMIT),
        name="na_attn",
    )(jnp.zeros((1,), jnp.int32), rpb_flat, qkvz, qkvz, qkvz)


def _post_kernel(x_ref, o_ref, z_ref, wo_ref, gp_ref, wg_ref, p_ref, wp_ref, gf_ref,
                 out_ref, *, feature_major, final_norm):
    if feature_major:
        z = z_ref[0].astype(F32)
        gated = (o_ref[0].astype(F32) * (z * _sigmoid(z))).astype(BF16)
        y = _dot_tn(gated, wo_ref[...])
    else:
        z = z_ref[...].astype(F32)
        gated = (o_ref[...].astype(F32) * (z * _sigmoid(z))).astype(BF16)
        y = _dot(gated, wo_ref[...])
    h = x_ref[...] + y
    hn = _rms(h, gp_ref[...]).astype(BF16)
    gate = _sigmoid(_dot(hn, wg_ref[...]))
    emb = _dot(p_ref[0].astype(BF16), wp_ref[...])
    xo = h + gate * emb
    if final_norm:
        xo = _rms(xo, gf_ref[...])
    out_ref[...] = xo


def _post_block(x2, o, z, z_col, w_out, g_ple, w_gate, p3, layer, w_proj, g_final, *,
                feature_major, final_norm, seq):
    m = x2.shape[0]
    tm = TM_PROJ
    per_seq = seq // tm
    if feature_major:
        oz_block = (1, D_MODEL, tm)
        o_spec = pl.BlockSpec(oz_block, lambda i: (i // per_seq, 0, i % per_seq))
        z_spec = pl.BlockSpec(oz_block, lambda i: (i // per_seq, 0, i % per_seq))
    else:
        o_spec = pl.BlockSpec((tm, D_MODEL), lambda i: (i, 0))
        z_spec = pl.BlockSpec((tm, D_MODEL), lambda i: (i, z_col))
    full = lambda shape: pl.BlockSpec(shape, lambda i: (0,) * len(shape))
    kern = functools.partial(_post_kernel, feature_major=feature_major, final_norm=final_norm)
    return pl.pallas_call(
        kern,
        grid=(m // tm,),
        in_specs=[
            pl.BlockSpec((tm, D_MODEL), lambda i: (i, 0)),
            o_spec,
            z_spec,
            full((D_MODEL, D_MODEL)),
            full((1, D_MODEL)),
            full((D_MODEL, D_MODEL)),
            pl.BlockSpec((1, tm, PLE_DIM), lambda i: (layer, i, 0)),
            full((PLE_DIM, D_MODEL)),
            full((1, D_MODEL)),
        ],
        out_specs=pl.BlockSpec((tm, D_MODEL), lambda i: (i, 0)),
        out_shape=jax.ShapeDtypeStruct((m, D_MODEL), F32),
        compiler_params=pltpu.CompilerParams(
            dimension_semantics=("arbitrary",),
            vmem_limit_bytes=VMEM_LIMIT),
        name="post_block_fm" if feature_major else "post_block",
    )(x2, o, z, w_out, g_ple, w_gate, p3, w_proj, g_final)


def _mla_inproj_kernel(x_ref, g_ref, wc_ref, wzt_ref, gq_ref, wq_ref,
                       gkv_ref, wk_ref, wvt_ref, cq_ref, sq_ref, ck_ref, sk_ref,
                       qn_ref, qpe_ref, kn_ref, kpe_ref, vt_ref, zt_ref):
    xn = _rms(x_ref[...], g_ref[...]).astype(BF16)
    c = _dot(xn, wc_ref[...])
    zt_ref[0] = _dot_nt(wzt_ref[...], xn).astype(BF16)

    cq = _rms(c[:, :MLA_Q_RANK], gq_ref[...]).astype(BF16)
    ckv = _rms(c[:, MLA_Q_RANK:MLA_Q_RANK + MLA_KV_RANK], gkv_ref[...]).astype(BF16)
    off = MLA_Q_RANK + MLA_KV_RANK
    kr = c[:, off:off + LANES]
    half = MLA_ROPE // 2
    lane = lax.broadcasted_iota(jnp.int32, (1, LANES), 1)
    kr_rot = jnp.where(lane % MLA_ROPE < half, -pltpu.roll(kr, LANES - half, 1), pltpu.roll(kr, half, 1))
    kpe_ref[...] = (kr * ck_ref[...] + kr_rot * sk_ref[...]).astype(BF16)

    qt = _dot_nt(wq_ref[...], cq)
    nope = MLA_HEADS * MLA_NOPE
    qn_ref[0] = qt[:nope].astype(BF16)
    cos_h, sin_h = cq_ref[:half], sq_ref[:half]
    for h in range(MLA_HEADS):
        x1 = qt[nope + h * MLA_ROPE:nope + h * MLA_ROPE + half]
        x2 = qt[nope + h * MLA_ROPE + half:nope + (h + 1) * MLA_ROPE]
        qpe_ref[0, h * MLA_ROPE:h * MLA_ROPE + half] = (x1 * cos_h - x2 * sin_h).astype(BF16)
        qpe_ref[0, h * MLA_ROPE + half:(h + 1) * MLA_ROPE] = (x2 * cos_h + x1 * sin_h).astype(BF16)

    kn_ref[...] = _dot(ckv, wk_ref[...]).astype(BF16)
    vt = _dot_nt(wvt_ref[...], ckv).astype(BF16)
    tm = vt.shape[1]
    pad_rows = lax.broadcasted_iota(jnp.int32, (MLA_V_PAD - MLA_V, MLA_BK), 0)
    ones_row = jnp.where(pad_rows == 0, 1.0, 0.0).astype(BF16)
    for h in range(MLA_HEADS):
        for s in range(tm // MLA_BK):
            vt_ref[0, h, s, :MLA_V] = vt[h * MLA_V:(h + 1) * MLA_V, s * MLA_BK:(s + 1) * MLA_BK]
            vt_ref[0, h, s, MLA_V:] = ones_row


def _mla_inproj(x2, g, wc, wzt, gq, wq, gkv, wk, wvt, cosq, sinq, cosk, sink, batch, seq):
    m = x2.shape[0]
    tm = TM_PROJ
    per_seq = seq // tm
    sub = tm // MLA_BK
    full = lambda a: pl.BlockSpec(a.shape, lambda i: (0,) * a.ndim)
    rowblk = lambda n: pl.BlockSpec((tm, n), lambda i: (i, 0))
    tabblk = lambda n: pl.BlockSpec((tm, n), lambda i: (i % per_seq, 0))
    fmtab = lambda n: pl.BlockSpec((n, tm), lambda i: (0, i % per_seq))
    fmblk = lambda n: pl.BlockSpec((1, n, tm), lambda i: (i // per_seq, 0, i % per_seq))
    nope = MLA_HEADS * MLA_NOPE
    pe = MLA_HEADS * MLA_ROPE
    return pl.pallas_call(
        _mla_inproj_kernel,
        grid=(m // tm,),
        in_specs=[rowblk(D_MODEL), full(g), full(wc), full(wzt), full(gq), full(wq),
                  full(gkv), full(wk), full(wvt),
                  fmtab(MLA_ROPE), fmtab(MLA_ROPE), tabblk(LANES), tabblk(LANES)],
        out_specs=[
            fmblk(nope), fmblk(pe), rowblk(nope), rowblk(LANES),
            pl.BlockSpec((1, MLA_HEADS, sub, MLA_V_PAD, MLA_BK),
                         lambda i: (i // per_seq, 0, i % per_seq, 0, 0)),
            pl.BlockSpec((1, D_MODEL, tm), lambda i: (i // per_seq, 0, i % per_seq)),
        ],
        out_shape=[
            jax.ShapeDtypeStruct((batch, nope, seq), BF16),
            jax.ShapeDtypeStruct((batch, pe, seq), BF16),
            jax.ShapeDtypeStruct((m, nope), BF16),
            jax.ShapeDtypeStruct((m, LANES), BF16),
            jax.ShapeDtypeStruct((batch, MLA_HEADS, seq // MLA_BK, MLA_V_PAD, MLA_BK), BF16),
            jax.ShapeDtypeStruct((batch, D_MODEL, seq), BF16),
        ],
        compiler_params=pltpu.CompilerParams(
            dimension_semantics=("arbitrary",),
            vmem_limit_bytes=VMEM_LIMIT),
        name="mla_inproj",
    )(x2, g, wc, wzt, gq, wq, gkv, wk, wvt, cosq, sinq, cosk, sink)


def _mla_attn_rescaling(q_scr, kn_ref, kpe_ref, vt_ref, ot_ref):
    nkb = kn_ref.shape[0] // MLA_BK
    for qb in range(MLA_QSTEPS):
        cols = slice(qb * MLA_BQ, (qb + 1) * MLA_BQ)

        def body(kb, carry, cols=cols):
            m, acc = carry
            off = pl.multiple_of(kb * MLA_BK, MLA_BK)
            k = jnp.concatenate([kn_ref[pl.ds(off, MLA_BK), :], kpe_ref[pl.ds(off, MLA_BK), :]], axis=1)
            st = _dot(k, q_scr[:, cols])
            m_new = jnp.maximum(m, jnp.max(st, axis=0, keepdims=True))
            p = jnp.exp2(st - m_new)
            acc = jnp.exp2(m - m_new) * acc + _dot(vt_ref[0, 0, kb], p.astype(BF16))
            return m_new, acc

        init = (jnp.full((1, MLA_BQ), -jnp.inf, F32), jnp.zeros((MLA_V_PAD, MLA_BQ), F32))
        _, acc = lax.fori_loop(0, nkb, body, init)
        ot_ref[0, :, cols] = (acc[:MLA_V] / acc[MLA_V:MLA_V + 1]).astype(BF16)


def _mla_attn_kernel(zero_ref, qn_ref, qpe_ref, kn_ref, kpe_ref, vt_ref, ot_ref, q_scr, s0_scr, s1_scr):
    h = pl.program_id(1)
    row = lax.broadcasted_iota(jnp.int32, (LANES, 1), 0)
    qn = qn_ref[0]
    qp = qpe_ref[0]
    q_scr[:LANES] = jnp.where(row // MLA_NOPE == h % (LANES // MLA_NOPE), qn, jnp.zeros_like(qn))
    q_scr[LANES:] = jnp.where(row // MLA_ROPE == h % (LANES // MLA_ROPE), qp, jnp.zeros_like(qp))
    nkb = kn_ref.shape[0] // MLA_BK

    z = zero_ref[0]
    s_bufs = (s0_scr, s1_scr)
    blocks = [(qb, kb) for qb in range(MLA_QSTEPS) for kb in range(nkb)]

    def scores(i):
        qb, kb = blocks[i]
        rows = slice(kb * MLA_BK, (kb + 1) * MLA_BK)
        k = jnp.concatenate([kn_ref[rows, :], kpe_ref[rows, :]], axis=1)
        s_bufs[i % 2][z] = _dot(k, q_scr[:, qb * MLA_BQ:(qb + 1) * MLA_BQ])

    finite = None
    scores(0)
    for i, (qb, kb) in enumerate(blocks):
        if i + 1 < len(blocks):
            scores(i + 1)
        if kb == 0:
            m_ref = jnp.max(s_bufs[i % 2][z], axis=0, keepdims=True)
            acc = jnp.zeros((MLA_V_PAD, MLA_BQ), F32)
        acc = acc + _dot(vt_ref[0, 0, kb], jnp.exp2(s_bufs[i % 2][z] - m_ref).astype(BF16))
        if kb == nkb - 1:
            denom = acc[MLA_V:MLA_V + 1]
            ot_ref[0, :, qb * MLA_BQ:(qb + 1) * MLA_BQ] = (acc[:MLA_V] / denom).astype(BF16)
            ok = jnp.max(denom) < MLA_DENOM_LIMIT
            finite = ok if finite is None else jnp.logical_and(finite, ok)

    @pl.when(jnp.logical_not(finite))
    def _():
        _mla_attn_rescaling(q_scr, kn_ref, kpe_ref, vt_ref, ot_ref)


def _mla_attn(qn, qpe, kn, kpe, vt, batch, seq):
    bq = MLA_QSTEPS * MLA_BQ
    per_n = LANES // MLA_NOPE
    per_r = LANES // MLA_ROPE
    return pl.pallas_call(
        _mla_attn_kernel,
        grid=(batch, MLA_HEADS, seq // bq),
        in_specs=[
            pl.BlockSpec(memory_space=pltpu.SMEM),
            pl.BlockSpec((1, LANES, bq), lambda b, h, i: (b, h // per_n, i)),
            pl.BlockSpec((1, LANES, bq), lambda b, h, i: (b, h // per_r, i)),
            pl.BlockSpec((seq, LANES), lambda b, h, i: (b, h // per_n)),
            pl.BlockSpec((seq, LANES), lambda b, h, i: (b, 0)),
            pl.BlockSpec((1, 1, seq // MLA_BK, MLA_V_PAD, MLA_BK), lambda b, h, i: (b, h, 0, 0, 0)),
        ],
        out_specs=pl.BlockSpec((1, MLA_V, bq), lambda b, h, i: (b, h, i)),
        out_shape=jax.ShapeDtypeStruct((batch, MLA_HEADS * MLA_V, seq), BF16),
        scratch_shapes=[pltpu.VMEM((2 * LANES, bq), BF16),
                        pltpu.VMEM((2, MLA_BK, MLA_BQ), F32),
                        pltpu.VMEM((2, MLA_BK, MLA_BQ), F32)],
        compiler_params=pltpu.CompilerParams(
            dimension_semantics=("arbitrary", "arbitrary", "arbitrary"),
            vmem_limit_bytes=VMEM_LIMIT),
        name="mla_attn",
    )(jnp.zeros((1,), jnp.int32), qn, qpe, kn, kpe, vt)


def _rope_tables(seq):
    inv = 1.0 / (ROPE_BASE ** (jnp.arange(0, MLA_ROPE, 2, dtype=F32) / MLA_ROPE))
    ang = jnp.arange(seq, dtype=F32)[:, None] * inv[None, :]
    cos = jnp.concatenate([jnp.cos(ang), jnp.cos(ang)], axis=-1)
    sin = jnp.concatenate([jnp.sin(ang), jnp.sin(ang)], axis=-1)
    return cos, sin


def kernel(x, p, norm_g, na_w_in, na_rpb, na_w_out, mla_w_in, mla_q_norm, mla_w_qb,
           mla_kv_norm, mla_w_kvb, mla_w_out, ple_norm, ple_w_gate, ple_w_proj, final_norm):
    batch, seq, d = x.shape
    m = batch * seq
    x2 = x.reshape(m, d)
    p3 = p.reshape(p.shape[0], m, PLE_DIM)
    row = lambda v: v.reshape(1, -1).astype(F32)

    hd = NA_HEADS * NA_HEAD_DIM
    q_scale = jnp.where(jnp.arange(4 * hd) < hd, NA_HEAD_DIM ** -0.5 * LOG2E, 1.0).astype(F32)
    w_in0 = (na_w_in.reshape(d, 4 * hd) * q_scale).astype(BF16)
    qkvz = _na_inproj(x2, row(norm_g[0]), w_in0)
    o0 = _na_attn(qkvz, na_rpb[0].reshape(-1).astype(F32), batch, seq)
    x2 = _post_block(x2, o0, qkvz, 3, na_w_out[0].astype(BF16), row(ple_norm[0]),
                     ple_w_gate[0].astype(BF16), p3, 0,
                     ple_w_proj[0].astype(BF16), row(final_norm),
                     feature_major=False, final_norm=False, seq=seq)

    w_in1 = mla_w_in[0]
    o1, o2 = MLA_Q_RANK, MLA_Q_RANK + MLA_KV_RANK
    o3 = o2 + MLA_ROPE
    w_kr = w_in1[:, o2:o3]
    rep = LANES // MLA_ROPE
    wc = jnp.concatenate([w_in1[:, :o2], jnp.tile(w_kr, (1, rep))], axis=1).astype(BF16)
    wzt = w_in1[:, o3:].T.astype(BF16)

    scale = (MLA_NOPE + MLA_ROPE) ** -0.5 * LOG2E
    wq3 = mla_w_qb[0].reshape(MLA_Q_RANK, MLA_HEADS, MLA_NOPE + MLA_ROPE) * scale
    wq_n = wq3[:, :, :MLA_NOPE].reshape(MLA_Q_RANK, -1)
    wq_p = wq3[:, :, MLA_NOPE:]
    wq = jnp.concatenate([wq_n, wq_p.reshape(MLA_Q_RANK, -1)], axis=1).T.astype(BF16)

    wkv3 = mla_w_kvb[0].reshape(MLA_KV_RANK, MLA_HEADS, MLA_NOPE + MLA_V)
    wk = wkv3[:, :, :MLA_NOPE].reshape(MLA_KV_RANK, -1).astype(BF16)
    wvt = wkv3[:, :, MLA_NOPE:].reshape(MLA_KV_RANK, -1).T.astype(BF16)

    cos, sin = _rope_tables(seq)
    cosq, sinq = cos.T, sin.T
    cosk, sink = jnp.tile(cos, (1, rep)), jnp.tile(sin, (1, rep))

    qn, qpe, kn, kpe, vt, zt = _mla_inproj(
        x2, row(norm_g[1]), wc, wzt, row(mla_q_norm[0]), wq, row(mla_kv_norm[0]),
        wk, wvt, cosq, sinq, cosk, sink, batch, seq)
    ot = _mla_attn(qn, qpe, kn, kpe, vt, batch, seq)
    out = _post_block(x2, ot, zt, 0, mla_w_out[0].astype(BF16), row(ple_norm[1]),
                      ple_w_gate[1].astype(BF16), p3, 1,
                      ple_w_proj[1].astype(BF16), row(final_norm),
                      feature_major=True, final_norm=True, seq=seq)
    return out.reshape(batch, seq, d)
```

```python
import functools

import jax
import jax.numpy as jnp
from jax import lax
from jax.experimental import pallas as pl
from jax.experimental.pallas import tpu as pltpu

D_MODEL = 1024
GRID_W = 64
NA_WIN_ROWS = 8
NA_WIN_COLS = 16
NA_HEADS = 16
NA_HEAD_DIM = 64
MLA_HEADS = 16
MLA_Q_RANK = 384
MLA_KV_RANK = 256
MLA_NOPE = 64
MLA_ROPE = 32
MLA_V = 64
MLA_V_PAD = 128
ROPE_BASE = 10000.0
PLE_DIM = 256
EPS = 1e-6

LANES = 128
NEG_BIG = -1e30
VMEM_LIMIT = 56 * 1024 * 1024

TM_PROJ = 1024
MLA_BQ = 512
MLA_QSTEPS = 8
MLA_BK = 512
MLA_DENOM_LIMIT = 3e38
LOG2E = 1.4426950408889634

BF16 = jnp.bfloat16
F32 = jnp.float32


def _rms(x, g):
    ms = jnp.mean(x * x, axis=-1, keepdims=True)
    return x * lax.rsqrt(ms + EPS) * g


def _sigmoid(x):
    return 0.5 * jnp.tanh(0.5 * x) + 0.5


def _dot(a, b):
    return jnp.dot(a, b, preferred_element_type=F32)


def _dot_nt(a, b):
    return lax.dot_general(a, b, (((1,), (1,)), ((), ())), preferred_element_type=F32)


def _dot_tn(a, b):
    return lax.dot_general(a, b, (((0,), (0,)), ((), ())), preferred_element_type=F32)


def _na_inproj_kernel(x_ref, g_ref, w_ref, o_ref, xn_ref):
    @pl.when(pl.program_id(1) == 0)
    def _():
        xn_ref[...] = _rms(x_ref[...], g_ref[...]).astype(BF16)

    o_ref[...] = _dot(xn_ref[...], w_ref[...]).astype(BF16)


def _na_inproj(x2, g, w):
    m = x2.shape[0]
    n = w.shape[1]
    tn = n
    return pl.pallas_call(
        _na_inproj_kernel,
        grid=(m // TM_PROJ, n // tn),
        in_specs=[
            pl.BlockSpec((TM_PROJ, D_MODEL), lambda i, j: (i, 0)),
            pl.BlockSpec((1, D_MODEL), lambda i, j: (0, 0)),
            pl.BlockSpec((D_MODEL, tn), lambda i, j: (0, j)),
        ],
        out_specs=pl.BlockSpec((TM_PROJ, tn), lambda i, j: (i, j)),
        out_shape=jax.ShapeDtypeStruct((m, n), BF16),
        scratch_shapes=[pltpu.VMEM((TM_PROJ, D_MODEL), BF16)],
        compiler_params=pltpu.CompilerParams(
            dimension_semantics=("arbitrary", "arbitrary"),
            vmem_limit_bytes=VMEM_LIMIT),
        name="na_inproj",
    )(x2, g, w)


NA_BAND = NA_WIN_ROWS * GRID_W
NA_SLABS = NA_HEADS * NA_HEAD_DIM // LANES
NA_ROW_OFFS = 2 * NA_WIN_ROWS - 1
NA_COL_OFFS = 2 * NA_WIN_COLS - 1
NA_PAIR_OFFS = NA_ROW_OFFS - 1
NA_ROWS_PER_STEP = 8


def _na_build_bias(rpb_ref, bias_scr):
    c = lax.broadcasted_iota(jnp.int32, (GRID_W, LANES), 0)
    lane = lax.broadcasted_iota(jnp.int32, (GRID_W, LANES), 1)
    j = lane % GRID_W
    cs = jnp.clip(c - NA_WIN_COLS // 2, 0, GRID_W - NA_WIN_COLS)
    valid = (j >= cs) & (j < cs + NA_WIN_COLS)
    diag = jnp.where(valid, j - c + (NA_WIN_COLS - 1), -1)
    second = lax.broadcasted_iota(jnp.int32, (1, LANES), 1) >= GRID_W

    def tile(idx, carry):
        h = idx // NA_PAIR_OFFS
        o = idx % NA_PAIR_OFFS
        base = (h * NA_ROW_OFFS + o) * NA_COL_OFFS
        acc = jnp.full((GRID_W, LANES), NEG_BIG, F32)
        for k in range(NA_COL_OFFS):
            val = jnp.where(second, rpb_ref[base + NA_COL_OFFS + k], rpb_ref[base + k]) * LOG2E
            acc = jnp.where(diag == k, val, acc)
        bias_scr[h, o] = acc
        return carry

    lax.fori_loop(0, NA_HEADS * NA_PAIR_OFFS, tile, 0)


def _na_attn_kernel(zero_ref, rpb_ref, q_ref, k_ref, v_ref, o_ref, bias_scr, s0_scr, s1_scr, p0_scr, p1_scr):
    first_row = pl.program_id(1) * NA_ROWS_PER_STEP

    @pl.when((pl.program_id(0) == 0) & (first_row == 0))
    def _():
        _na_build_bias(rpb_ref, bias_scr)

    z = zero_ref[0]
    s_bufs = (s0_scr, s1_scr)
    p_bufs = (p0_scr, p1_scr)
    rows = k_ref.shape[0] // GRID_W
    first = lax.broadcasted_iota(jnp.int32, (1, LANES), 1) < NA_HEAD_DIM
    starts, offs = [], []
    for rr in range(NA_ROWS_PER_STEP):
        r = first_row + rr
        rs = jnp.clip(r - NA_WIN_ROWS // 2, 0, rows - NA_WIN_ROWS)
        starts.append(pl.multiple_of(rs * GRID_W, GRID_W))
        offs.append((NA_WIN_ROWS - 1) - (r - rs))
    work = [(rr, slab) for rr in range(NA_ROWS_PER_STEP) for slab in range(NA_SLABS)]

    def scores(i):
        rr, slab = work[i]
        cols = slice(slab * LANES, (slab + 1) * LANES)
        qs = q_ref[rr * GRID_W:(rr + 1) * GRID_W, cols]
        zero = jnp.zeros_like(qs)
        q2 = jnp.concatenate([jnp.where(first, qs, zero), jnp.where(first, zero, qs)], axis=0)
        s_bufs[i % 2][z] = _dot_nt(q2, k_ref[pl.ds(starts[rr], NA_BAND), cols])

    def softmax(i):
        rr, slab = work[i]
        bias = jnp.concatenate(
            [jnp.concatenate([bias_scr[2 * slab + e, offs[rr] + 2 * a] for a in range(NA_WIN_ROWS // 2)], axis=1)
             for e in range(2)], axis=0)
        s = s_bufs[i % 2][z] + bias
        m = jnp.max(s, axis=-1, keepdims=True)
        p_bufs[i % 2][z] = jnp.exp2(s - m).astype(BF16)

    ones = jnp.ones((NA_BAND, LANES), BF16)

    def values(i):
        rr, slab = work[i]
        cols = slice(slab * LANES, (slab + 1) * LANES)
        v1 = jnp.concatenate([v_ref[pl.ds(starts[rr], NA_BAND), cols], ones], axis=1)
        pv = _dot(p_bufs[i % 2][z], v1)
        pv = pv[:, :LANES] / pv[:, LANES:]
        o_ref[rr * GRID_W:(rr + 1) * GRID_W, cols] = jnp.where(first, pv[:GRID_W], pv[GRID_W:]).astype(BF16)

    n = len(work)
    scores(0)
    scores(1)
    softmax(0)
    for i in range(n):
        if i + 2 < n:
            scores(i + 2)
        if i + 1 < n:
            softmax(i + 1)
        values(i)


def _na_attn(qkvz, rpb_flat, batch, seq):
    hd = NA_HEADS * NA_HEAD_DIM
    steps = seq // GRID_W // NA_ROWS_PER_STEP
    tq = NA_ROWS_PER_STEP * GRID_W
    return pl.pallas_call(
        _na_attn_kernel,
        grid=(batch, steps),
        in_specs=[
            pl.BlockSpec(memory_space=pltpu.SMEM),
            pl.BlockSpec(memory_space=pltpu.SMEM),
            pl.BlockSpec((tq, hd), lambda b, r: (b * steps + r, 0)),
            pl.BlockSpec((seq, hd), lambda b, r: (b, 1)),
            pl.BlockSpec((seq, hd), lambda b, r: (b, 2)),
        ],
        out_specs=pl.BlockSpec((tq, hd), lambda b, r: (b * steps + r, 0)),
        out_shape=jax.ShapeDtypeStruct((batch * seq, hd), BF16),
        scratch_shapes=[pltpu.VMEM((NA_HEADS, NA_PAIR_OFFS, GRID_W, LANES), F32),
                        pltpu.VMEM((2, 2 * GRID_W, NA_BAND), F32),
                        pltpu.VMEM((2, 2 * GRID_W, NA_BAND), F32),
                        pltpu.VMEM((2, 2 * GRID_W, NA_BAND), BF16),
                        pltpu.VMEM((2, 2 * GRID_W, NA_BAND), BF16)],
        compiler_params=pltpu.CompilerParams(
            dimension_semantics=("arbitrary", "arbitrary"),
            vmem_limit_bytes=VMEM_LIMIT),
        name="na_attn",
    )(jnp.zeros((1,), jnp.int32), rpb_flat, qkvz, qkvz, qkvz)


def _post_kernel(x_ref, o_ref, z_ref, wo_ref, gp_ref, wg_ref, p_ref, wp_ref, gf_ref,
                 out_ref, *, feature_major, final_norm):
    if feature_major:
        z = z_ref[0].astype(F32)
        gated = (o_ref[0].astype(F32) * (z * _sigmoid(z))).astype(BF16)
        y = _dot_tn(gated, wo_ref[...])
    else:
        z = z_ref[...].astype(F32)
        gated = (o_ref[...].astype(F32) * (z * _sigmoid(z))).astype(BF16)
        y = _dot(gated, wo_ref[...])
    h = x_ref[...] + y
    hn = _rms(h, gp_ref[...]).astype(BF16)
    gate = _sigmoid(_dot(hn, wg_ref[...]))
    emb = _dot(p_ref[0].astype(BF16), wp_ref[...])
    xo = h + gate * emb
    if final_norm:
        xo = _rms(xo, gf_ref[...])
    out_ref[...] = xo


def _post_block(x2, o, z, z_col, w_out, g_ple, w_gate, p3, layer, w_proj, g_final, *,
                feature_major, final_norm, seq):
    m = x2.shape[0]
    tm = TM_PROJ
    per_seq = seq // tm
    if feature_major:
        oz_block = (1, D_MODEL, tm)
        o_spec = pl.BlockSpec(oz_block, lambda i: (i // per_seq, 0, i % per_seq))
        z_spec = pl.BlockSpec(oz_block, lambda i: (i // per_seq, 0, i % per_seq))
    else:
        o_spec = pl.BlockSpec((tm, D_MODEL), lambda i: (i, 0))
        z_spec = pl.BlockSpec((tm, D_MODEL), lambda i: (i, z_col))
    full = lambda shape: pl.BlockSpec(shape, lambda i: (0,) * len(shape))
    kern = functools.partial(_post_kernel, feature_major=feature_major, final_norm=final_norm)
    return pl.pallas_call(
        kern,
        grid=(m // tm,),
        in_specs=[
            pl.BlockSpec((tm, D_MODEL), lambda i: (i, 0)),
            o_spec,
            z_spec,
            full((D_MODEL, D_MODEL)),
            full((1, D_MODEL)),
            full((D_MODEL, D_MODEL)),
            pl.BlockSpec((1, tm, PLE_DIM), lambda i: (layer, i, 0)),
            full((PLE_DIM, D_MODEL)),
            full((1, D_MODEL)),
        ],
        out_specs=pl.BlockSpec((tm, D_MODEL), lambda i: (i, 0)),
        out_shape=jax.ShapeDtypeStruct((m, D_MODEL), F32),
        compiler_params=pltpu.CompilerParams(
            dimension_semantics=("arbitrary",),
            vmem_limit_bytes=VMEM_LIMIT),
        name="post_block_fm" if feature_major else "post_block",
    )(x2, o, z, w_out, g_ple, w_gate, p3, w_proj, g_final)


def _mla_inproj_kernel(x_ref, g_ref, wc_ref, wzt_ref, gq_ref, wq_ref,
                       gkv_ref, wk_ref, wvt_ref, cq_ref, sq_ref, ck_ref, sk_ref,
                       qn_ref, qpe_ref, kn_ref, kpe_ref, vt_ref, zt_ref):
    xn = _rms(x_ref[...], g_ref[...]).astype(BF16)
    c = _dot(xn, wc_ref[...])
    zt_ref[0] = _dot_nt(wzt_ref[...], xn).astype(BF16)

    cq = _rms(c[:, :MLA_Q_RANK], gq_ref[...]).astype(BF16)
    ckv = _rms(c[:, MLA_Q_RANK:MLA_Q_RANK + MLA_KV_RANK], gkv_ref[...]).astype(BF16)
    off = MLA_Q_RANK + MLA_KV_RANK
    kr = c[:, off:off + LANES]
    half = MLA_ROPE // 2
    lane = lax.broadcasted_iota(jnp.int32, (1, LANES), 1)
    kr_rot = jnp.where(lane % MLA_ROPE < half, -pltpu.roll(kr, LANES - half, 1), pltpu.roll(kr, half, 1))
    kpe_ref[...] = (kr * ck_ref[...] + kr_rot * sk_ref[...]).astype(BF16)

    qt = _dot_nt(wq_ref[...], cq)
    nope = MLA_HEADS * MLA_NOPE
    qn_ref[0] = qt[:nope].astype(BF16)
    cos_h, sin_h = cq_ref[:half], sq_ref[:half]
    for h in range(MLA_HEADS):
        x1 = qt[nope + h * MLA_ROPE:nope + h * MLA_ROPE + half]
        x2 = qt[nope + h * MLA_ROPE + half:nope + (h + 1) * MLA_ROPE]
        qpe_ref[0, h * MLA_ROPE:h * MLA_ROPE + half] = (x1 * cos_h - x2 * sin_h).astype(BF16)
        qpe_ref[0, h * MLA_ROPE + half:(h + 1) * MLA_ROPE] = (x2 * cos_h + x1 * sin_h).astype(BF16)

    kn_ref[...] = _dot(ckv, wk_ref[...]).astype(BF16)
    vt = _dot_nt(wvt_ref[...], ckv).astype(BF16)
    tm = vt.shape[1]
    pad_rows = lax.broadcasted_iota(jnp.int32, (MLA_V_PAD - MLA_V, MLA_BK), 0)
    ones_row = jnp.where(pad_rows == 0, 1.0, 0.0).astype(BF16)
    for h in range(MLA_HEADS):
        for s in range(tm // MLA_BK):
            vt_ref[0, h, s, :MLA_V] = vt[h * MLA_V:(h + 1) * MLA_V, s * MLA_BK:(s + 1) * MLA_BK]
            vt_ref[0, h, s, MLA_V:] = ones_row


def _mla_inproj(x2, g, wc, wzt, gq, wq, gkv, wk, wvt, cosq, sinq, cosk, sink, batch, seq):
    m = x2.shape[0]
    tm = TM_PROJ
    per_seq = seq // tm
    sub = tm // MLA_BK
    full = lambda a: pl.BlockSpec(a.shape, lambda i: (0,) * a.ndim)
    rowblk = lambda n: pl.BlockSpec((tm, n), lambda i: (i, 0))
    tabblk = lambda n: pl.BlockSpec((tm, n), lambda i: (i % per_seq, 0))
    fmtab = lambda n: pl.BlockSpec((n, tm), lambda i: (0, i % per_seq))
    fmblk = lambda n: pl.BlockSpec((1, n, tm), lambda i: (i // per_seq, 0, i % per_seq))
    nope = MLA_HEADS * MLA_NOPE
    pe = MLA_HEADS * MLA_ROPE
    return pl.pallas_call(
        _mla_inproj_kernel,
        grid=(m // tm,),
        in_specs=[rowblk(D_MODEL), full(g), full(wc), full(wzt), full(gq), full(wq),
                  full(gkv), full(wk), full(wvt),
                  fmtab(MLA_ROPE), fmtab(MLA_ROPE), tabblk(LANES), tabblk(LANES)],
        out_specs=[
            fmblk(nope), fmblk(pe), rowblk(nope), rowblk(LANES),
            pl.BlockSpec((1, MLA_HEADS, sub, MLA_V_PAD, MLA_BK),
                         lambda i: (i // per_seq, 0, i % per_seq, 0, 0)),
            pl.BlockSpec((1, D_MODEL, tm), lambda i: (i // per_seq, 0, i % per_seq)),
        ],
        out_shape=[
            jax.ShapeDtypeStruct((batch, nope, seq), BF16),
            jax.ShapeDtypeStruct((batch, pe, seq), BF16),
            jax.ShapeDtypeStruct((m, nope), BF16),
            jax.ShapeDtypeStruct((m, LANES), BF16),
            jax.ShapeDtypeStruct((batch, MLA_HEADS, seq // MLA_BK, MLA_V_PAD, MLA_BK), BF16),
            jax.ShapeDtypeStruct((batch, D_MODEL, seq), BF16),
        ],
        compiler_params=pltpu.CompilerParams(
            dimension_semantics=("arbitrary",),
            vmem_limit_bytes=VMEM_LIMIT),
        name="mla_inproj",
    )(x2, g, wc, wzt, gq, wq, gkv, wk, wvt, cosq, sinq, cosk, sink)


def _mla_attn_rescaling(q_scr, kn_ref, kpe_ref, vt_ref, ot_ref):
    nkb = kn_ref.shape[0] // MLA_BK
    for qb in range(MLA_QSTEPS):
        cols = slice(qb * MLA_BQ, (qb + 1) * MLA_BQ)

        def body(kb, carry, cols=cols):
            m, acc = carry
            off = pl.multiple_of(kb * MLA_BK, MLA_BK)
            k = jnp.concatenate([kn_ref[pl.ds(off, MLA_BK), :], kpe_ref[pl.ds(off, MLA_BK), :]], axis=1)
            st = _dot(k, q_scr[:, cols])
            m_new = jnp.maximum(m, jnp.max(st, axis=0, keepdims=True))
            p = jnp.exp2(st - m_new)
            acc = jnp.exp2(m - m_new) * acc + _dot(vt_ref[0, 0, kb], p.astype(BF16))
            return m_new, acc

        init = (jnp.full((1, MLA_BQ), -jnp.inf, F32), jnp.zeros((MLA_V_PAD, MLA_BQ), F32))
        _, acc = lax.fori_loop(0, nkb, body, init)
        ot_ref[0, :, cols] = (acc[:MLA_V] / acc[MLA_V:MLA_V + 1]).astype(BF16)


def _mla_attn_kernel(zero_ref, qn_ref, qpe_ref, kn_ref, kpe_ref, vt_ref, ot_ref, q_scr, s0_scr, s1_scr):
    h = pl.program_id(1)
    row = lax.broadcasted_iota(jnp.int32, (LANES, 1), 0)
    qn = qn_ref[0]
    qp = qpe_ref[0]
    q_scr[:LANES] = jnp.where(row // MLA_NOPE == h % (LANES // MLA_NOPE), qn, jnp.zeros_like(qn))
    q_scr[LANES:] = jnp.where(row // MLA_ROPE == h % (LANES // MLA_ROPE), qp, jnp.zeros_like(qp))
    nkb = kn_ref.shape[0] // MLA_BK

    z = zero_ref[0]
    s_bufs = (s0_scr, s1_scr)
    blocks = [(qb, kb) for qb in range(MLA_QSTEPS) for kb in range(nkb)]

    def scores(i):
        qb, kb = blocks[i]
        rows = slice(kb * MLA_BK, (kb + 1) * MLA_BK)
        k = jnp.concatenate([kn_ref[rows, :], kpe_ref[rows, :]], axis=1)
        s_bufs[i % 2][z] = _dot(k, q_scr[:, qb * MLA_BQ:(qb + 1) * MLA_BQ])

    finite = None
    scores(0)
    for i, (qb, kb) in enumerate(blocks):
        if i + 1 < len(blocks):
            scores(i + 1)
        if kb == 0:
            m_ref = jnp.max(s_bufs[i % 2][z], axis=0, keepdims=True)
            acc = jnp.zeros((MLA_V_PAD, MLA_BQ), F32)
        acc = acc + _dot(vt_ref[0, 0, kb], jnp.exp2(s_bufs[i % 2][z] - m_ref).astype(BF16))
        if kb == nkb - 1:
            denom = acc[MLA_V:MLA_V + 1]
            ot_ref[0, :, qb * MLA_BQ:(qb + 1) * MLA_BQ] = (acc[:MLA_V] / denom).astype(BF16)
            ok = jnp.max(denom) < MLA_DENOM_LIMIT
            finite = ok if finite is None else jnp.logical_and(finite, ok)

    @pl.when(jnp.logical_not(finite))
    def _():
        _mla_attn_rescaling(q_scr, kn_ref, kpe_ref, vt_ref, ot_ref)


def _mla_attn(qn, qpe, kn, kpe, vt, batch, seq):
    bq = MLA_QSTEPS * MLA_BQ
    per_n = LANES // MLA_NOPE
    per_r = LANES // MLA_ROPE
    return pl.pallas_call(
        _mla_attn_kernel,
        grid=(batch, MLA_HEADS, seq // bq),
        in_specs=[
            pl.BlockSpec(memory_space=pltpu.SMEM),
            pl.BlockSpec((1, LANES, bq), lambda b, h, i: (b, h // per_n, i)),
            pl.BlockSpec((1, LANES, bq), lambda b, h, i: (b, h // per_r, i)),
            pl.BlockSpec((seq, LANES), lambda b, h, i: (b, h // per_n)),
            pl.BlockSpec((seq, LANES), lambda b, h, i: (b, 0)),
            pl.BlockSpec((1, 1, seq // MLA_BK, MLA_V_PAD, MLA_BK), lambda b, h, i: (b, h, 0, 0, 0)),
        ],
        out_specs=pl.BlockSpec((1, MLA_V, bq), lambda b, h, i: (b, h, i)),
        out_shape=jax.ShapeDtypeStruct((batch, MLA_HEADS * MLA_V, seq), BF16),
        scratch_shapes=[pltpu.VMEM((2 * LANES, bq), BF16),
                        pltpu.VMEM((2, MLA_BK, MLA_BQ), F32),
                        pltpu.VMEM((2, MLA_BK, MLA_BQ), F32)],
        compiler_params=pltpu.CompilerParams(
            dimension_semantics=("arbitrary", "arbitrary", "arbitrary"),
            vmem_limit_bytes=VMEM_LIMIT),
        name="mla_attn",
    )(jnp.zeros((1,), jnp.int32), qn, qpe, kn, kpe, vt)


def _rope_tables(seq):
    inv = 1.0 / (ROPE_BASE ** (jnp.arange(0, MLA_ROPE, 2, dtype=F32) / MLA_ROPE))
    ang = jnp.arange(seq, dtype=F32)[:, None] * inv[None, :]
    cos = jnp.concatenate([jnp.cos(ang), jnp.cos(ang)], axis=-1)
    sin = jnp.concatenate([jnp.sin(ang), jnp.sin(ang)], axis=-1)
    return cos, sin


def kernel(x, p, norm_g, na_w_in, na_rpb, na_w_out, mla_w_in, mla_q_norm, mla_w_qb,
           mla_kv_norm, mla_w_kvb, mla_w_out, ple_norm, ple_w_gate, ple_w_proj, final_norm):
    batch, seq, d = x.shape
    m = batch * seq
    x2 = x.reshape(m, d)
    p3 = p.reshape(p.shape[0], m, PLE_DIM)
    row = lambda v: v.reshape(1, -1).astype(F32)

    hd = NA_HEADS * NA_HEAD_DIM
    q_scale = jnp.where(jnp.arange(4 * hd) < hd, NA_HEAD_DIM ** -0.5 * LOG2E, 1.0).astype(F32)
    w_in0 = (na_w_in.reshape(d, 4 * hd) * q_scale).astype(BF16)
    qkvz = _na_inproj(x2, row(norm_g[0]), w_in0)
    o0 = _na_attn(qkvz, na_rpb[0].reshape(-1).astype(F32), batch, seq)
    x2 = _post_block(x2, o0, qkvz, 3, na_w_out[0].astype(BF16), row(ple_norm[0]),
                     ple_w_gate[0].astype(BF16), p3, 0,
                     ple_w_proj[0].astype(BF16), row(final_norm),
                     feature_major=False, final_norm=False, seq=seq)

    w_in1 = mla_w_in[0]
    o1, o2 = MLA_Q_RANK, MLA_Q_RANK + MLA_KV_RANK
    o3 = o2 + MLA_ROPE
    w_kr = w_in1[:, o2:o3]
    rep = LANES // MLA_ROPE
    wc = jnp.concatenate([w_in1[:, :o2], jnp.tile(w_kr, (1, rep))], axis=1).astype(BF16)
    wzt = w_in1[:, o3:].T.astype(BF16)

    scale = (MLA_NOPE + MLA_ROPE) ** -0.5 * LOG2E
    wq3 = mla_w_qb[0].reshape(MLA_Q_RANK, MLA_HEADS, MLA_NOPE + MLA_ROPE) * scale
    wq_n = wq3[:, :, :MLA_NOPE].reshape(MLA_Q_RANK, -1)
    wq_p = wq3[:, :, MLA_NOPE:]
    wq = jnp.concatenate([wq_n, wq_p.reshape(MLA_Q_RANK, -1)], axis=1).T.astype(BF16)

    wkv3 = mla_w_kvb[0].reshape(MLA_KV_RANK, MLA_HEADS, MLA_NOPE + MLA_V)
    wk = wkv3[:, :, :MLA_NOPE].reshape(MLA_KV_RANK, -1).astype(BF16)
    wvt = wkv3[:, :, MLA_NOPE:].reshape(MLA_KV_RANK, -1).T.astype(BF16)

    cos, sin = _rope_tables(seq)
    cosq, sinq = cos.T, sin.T
    cosk, sink = jnp.tile(cos, (1, rep)), jnp.tile(sin, (1, rep))

    qn, qpe, kn, kpe, vt, zt = _mla_inproj(
        x2, row(norm_g[1]), wc, wzt, row(mla_q_norm[0]), wq, row(mla_kv_norm[0]),
        wk, wvt, cosq, sinq, cosk, sink, batch, seq)
    ot = _mla_attn(qn, qpe, kn, kpe, vt, batch, seq)
    out = _post_block(x2, ot, zt, 0, mla_w_out[0].astype(BF16), row(ple_norm[1]),
                      ple_w_gate[1].astype(BF16), p3, 1,
                      ple_w_proj[1].astype(BF16), row(final_norm),
                      feature_major=True, final_norm=True, seq=seq)
    return out.reshape(batch, seq, d)
```

```python
import functools

import jax
import jax.numpy as jnp
from jax import lax
from jax.experimental import pallas as pl
from jax.experimental.pallas import tpu as pltpu

D_MODEL = 1024
GRID_W = 64
NA_WIN_ROWS = 8
NA_WIN_COLS = 16
NA_HEADS = 16
NA_HEAD_DIM = 64
MLA_HEADS = 16
MLA_Q_RANK = 384
MLA_KV_RANK = 256
MLA_NOPE = 64
MLA_ROPE = 32
MLA_V = 64
MLA_V_PAD = 128
ROPE_BASE = 10000.0
PLE_DIM = 256
EPS = 1e-6

LANES = 128
NEG_BIG = -1e30
VMEM_LIMIT = 56 * 1024 * 1024

TM_PROJ = 1024
MLA_BQ = 512
MLA_QSTEPS = 8
MLA_BK = 512
MLA_DENOM_LIMIT = 3e38
LOG2E = 1.4426950408889634

BF16 = jnp.bfloat16
F32 = jnp.float32


def _rms(x, g):
    ms = jnp.mean(x * x, axis=-1, keepdims=True)
    return x * lax.rsqrt(ms + EPS) * g


def _sigmoid(x):
    return 0.5 * jnp.tanh(0.5 * x) + 0.5


def _dot(a, b):
    return jnp.dot(a, b, preferred_element_type=F32)


def _dot_nt(a, b):
    return lax.dot_general(a, b, (((1,), (1,)), ((), ())), preferred_element_type=F32)


def _dot_tn(a, b):
    return lax.dot_general(a, b, (((0,), (0,)), ((), ())), preferred_element_type=F32)


def _na_inproj_kernel(x_ref, g_ref, w_ref, o_ref, xn_ref):
    @pl.when(pl.program_id(1) == 0)
    def _():
        xn_ref[...] = _rms(x_ref[...], g_ref[...]).astype(BF16)

    o_ref[...] = _dot(xn_ref[...], w_ref[...]).astype(BF16)


def _na_inproj(x2, g, w):
    m = x2.shape[0]
    n = w.shape[1]
    tn = n
    return pl.pallas_call(
        _na_inproj_kernel,
        grid=(m // TM_PROJ, n // tn),
        in_specs=[
            pl.BlockSpec((TM_PROJ, D_MODEL), lambda i, j: (i, 0)),
            pl.BlockSpec((1, D_MODEL), lambda i, j: (0, 0)),
            pl.BlockSpec((D_MODEL, tn), lambda i, j: (0, j)),
        ],
        out_specs=pl.BlockSpec((TM_PROJ, tn), lambda i, j: (i, j)),
        out_shape=jax.ShapeDtypeStruct((m, n), BF16),
        scratch_shapes=[pltpu.VMEM((TM_PROJ, D_MODEL), BF16)],
        compiler_params=pltpu.CompilerParams(
            dimension_semantics=("arbitrary", "arbitrary"),
            vmem_limit_bytes=VMEM_LIMIT),
        name="na_inproj",
    )(x2, g, w)


NA_BAND = NA_WIN_ROWS * GRID_W
NA_SLABS = NA_HEADS * NA_HEAD_DIM // LANES
NA_ROW_OFFS = 2 * NA_WIN_ROWS - 1
NA_COL_OFFS = 2 * NA_WIN_COLS - 1
NA_PAIR_OFFS = NA_ROW_OFFS - 1
NA_ROWS_PER_STEP = 8


def _na_build_bias(rpb_ref, bias_scr):
    c = lax.broadcasted_iota(jnp.int32, (GRID_W, LANES), 0)
    lane = lax.broadcasted_iota(jnp.int32, (GRID_W, LANES), 1)
    j = lane % GRID_W
    cs = jnp.clip(c - NA_WIN_COLS // 2, 0, GRID_W - NA_WIN_COLS)
    valid = (j >= cs) & (j < cs + NA_WIN_COLS)

    def tile(idx, carry):
        h = idx // NA_PAIR_OFFS
        o = idx % NA_PAIR_OFFS
        row = h * NA_ROW_OFFS + o
        vec = jnp.maximum(rpb_ref[pl.ds(row, 1), :], pltpu.roll(rpb_ref[pl.ds(row + 1, 1), :], GRID_W, 1))
        vec = jnp.broadcast_to(vec * LOG2E, (GRID_W, LANES))
        rolled = pltpu.roll(vec, LANES - (NA_WIN_COLS - 1), 1, stride=1, stride_axis=0)
        bias_scr[h, o] = jnp.where(valid, rolled, NEG_BIG)
        return carry

    lax.fori_loop(0, NA_HEADS * NA_PAIR_OFFS, tile, 0, unroll=8)


def _na_attn_kernel(zero_ref, rpb_ref, q_ref, k_ref, v_ref, o_ref, bias_scr, s0_scr, s1_scr, p0_scr, p1_scr):
    first_row = pl.program_id(1) * NA_ROWS_PER_STEP

    @pl.when((pl.program_id(0) == 0) & (first_row == 0))
    def _():
        _na_build_bias(rpb_ref, bias_scr)

    z = zero_ref[0]
    s_bufs = (s0_scr, s1_scr)
    p_bufs = (p0_scr, p1_scr)
    rows = k_ref.shape[0] // GRID_W
    first = lax.broadcasted_iota(jnp.int32, (1, LANES), 1) < NA_HEAD_DIM
    starts, offs = [], []
    for rr in range(NA_ROWS_PER_STEP):
        r = first_row + rr
        rs = jnp.clip(r - NA_WIN_ROWS // 2, 0, rows - NA_WIN_ROWS)
        starts.append(pl.multiple_of(rs * GRID_W, GRID_W))
        offs.append((NA_WIN_ROWS - 1) - (r - rs))
    work = [(rr, slab) for rr in range(NA_ROWS_PER_STEP) for slab in range(NA_SLABS)]

    def scores(i):
        rr, slab = work[i]
        cols = slice(slab * LANES, (slab + 1) * LANES)
        qs = q_ref[rr * GRID_W:(rr + 1) * GRID_W, cols]
        zero = jnp.zeros_like(qs)
        q2 = jnp.concatenate([jnp.where(first, qs, zero), jnp.where(first, zero, qs)], axis=0)
        s_bufs[i % 2][z] = _dot_nt(q2, k_ref[pl.ds(starts[rr], NA_BAND), cols])

    def softmax(i):
        rr, slab = work[i]
        bias = jnp.concatenate(
            [jnp.concatenate([bias_scr[2 * slab + e, offs[rr] + 2 * a] for a in range(NA_WIN_ROWS // 2)], axis=1)
             for e in range(2)], axis=0)
        s = s_bufs[i % 2][z] + bias
        m = jnp.max(s, axis=-1, keepdims=True)
        p_bufs[i % 2][z] = jnp.exp2(s - m).astype(BF16)

    ones = jnp.ones((NA_BAND, LANES), BF16)

    def values(i):
        rr, slab = work[i]
        cols = slice(slab * LANES, (slab + 1) * LANES)
        v1 = jnp.concatenate([v_ref[pl.ds(starts[rr], NA_BAND), cols], ones], axis=1)
        pv = _dot(p_bufs[i % 2][z], v1)
        pv = pv[:, :LANES] / pv[:, LANES:]
        o_ref[rr * GRID_W:(rr + 1) * GRID_W, cols] = jnp.where(first, pv[:GRID_W], pv[GRID_W:]).astype(BF16)

    n = len(work)
    scores(0)
    scores(1)
    softmax(0)
    for i in range(n):
        if i + 2 < n:
            scores(i + 2)
        if i + 1 < n:
            softmax(i + 1)
        values(i)


def _na_attn(qkvz, rpb_rows, batch, seq):
    hd = NA_HEADS * NA_HEAD_DIM
    steps = seq // GRID_W // NA_ROWS_PER_STEP
    tq = NA_ROWS_PER_STEP * GRID_W
    return pl.pallas_call(
        _na_attn_kernel,
        grid=(batch, steps),
        in_specs=[
            pl.BlockSpec(memory_space=pltpu.SMEM),
            pl.BlockSpec(rpb_rows.shape, lambda b, r: (0, 0)),
            pl.BlockSpec((tq, hd), lambda b, r: (b * steps + r, 0)),
            pl.BlockSpec((seq, hd), lambda b, r: (b, 1)),
            pl.BlockSpec((seq, hd), lambda b, r: (b, 2)),
        ],
        out_specs=pl.BlockSpec((tq, hd), lambda b, r: (b * steps + r, 0)),
        out_shape=jax.ShapeDtypeStruct((batch * seq, hd), BF16),
        scratch_shapes=[pltpu.VMEM((NA_HEADS, NA_PAIR_OFFS, GRID_W, LANES), F32),
                        pltpu.VMEM((2, 2 * GRID_W, NA_BAND), F32),
                        pltpu.VMEM((2, 2 * GRID_W, NA_BAND), F32),
                        pltpu.VMEM((2, 2 * GRID_W, NA_BAND), BF16),
                        pltpu.VMEM((2, 2 * GRID_W, NA_BAND), BF16)],
        compiler_params=pltpu.CompilerParams(
            dimension_semantics=("arbitrary", "arbitrary"),
            vmem_limit_bytes=VMEM_LIMIT),
        name="na_attn",
    )(jnp.zeros((1,), jnp.int32), rpb_rows, qkvz, qkvz, qkvz)


def _post_kernel(x_ref, o_ref, z_ref, wo_ref, gp_ref, wg_ref, p_ref, wp_ref, gf_ref,
                 out_ref, *, feature_major, final_norm):
    if feature_major:
        z = z_ref[0].astype(F32)
        gated = (o_ref[0].astype(F32) * (z * _sigmoid(z))).astype(BF16)
        y = _dot_tn(gated, wo_ref[...])
    else:
        z = z_ref[...].astype(F32)
        gated = (o_ref[...].astype(F32) * (z * _sigmoid(z))).astype(BF16)
        y = _dot(gated, wo_ref[...])
    h = x_ref[...] + y
    hn = _rms(h, gp_ref[...]).astype(BF16)
    gate = _sigmoid(_dot(hn, wg_ref[...]))
    emb = _dot(p_ref[0].astype(BF16), wp_ref[...])
    xo = h + gate * emb
    if final_norm:
        xo = _rms(xo, gf_ref[...])
    out_ref[...] = xo


def _post_block(x2, o, z, z_col, w_out, g_ple, w_gate, p3, layer, w_proj, g_final, *,
                feature_major, final_norm, seq):
    m = x2.shape[0]
    tm = TM_PROJ
    per_seq = seq // tm
    if feature_major:
        oz_block = (1, D_MODEL, tm)
        o_spec = pl.BlockSpec(oz_block, lambda i: (i // per_seq, 0, i % per_seq))
        z_spec = pl.BlockSpec(oz_block, lambda i: (i // per_seq, 0, i % per_seq))
    else:
        o_spec = pl.BlockSpec((tm, D_MODEL), lambda i: (i, 0))
        z_spec = pl.BlockSpec((tm, D_MODEL), lambda i: (i, z_col))
    full = lambda shape: pl.BlockSpec(shape, lambda i: (0,) * len(shape))
    kern = functools.partial(_post_kernel, feature_major=feature_major, final_norm=final_norm)
    return pl.pallas_call(
        kern,
        grid=(m // tm,),
        in_specs=[
            pl.BlockSpec((tm, D_MODEL), lambda i: (i, 0)),
            o_spec,
            z_spec,
            full((D_MODEL, D_MODEL)),
            full((1, D_MODEL)),
            full((D_MODEL, D_MODEL)),
            pl.BlockSpec((1, tm, PLE_DIM), lambda i: (layer, i, 0)),
            full((PLE_DIM, D_MODEL)),
            full((1, D_MODEL)),
        ],
        out_specs=pl.BlockSpec((tm, D_MODEL), lambda i: (i, 0)),
        out_shape=jax.ShapeDtypeStruct((m, D_MODEL), F32),
        compiler_params=pltpu.CompilerParams(
            dimension_semantics=("arbitrary",),
            vmem_limit_bytes=VMEM_LIMIT),
        name="post_block_fm" if feature_major else "post_block",
    )(x2, o, z, w_out, g_ple, w_gate, p3, w_proj, g_final)


def _mla_inproj_kernel(x_ref, g_ref, wc_ref, wzt_ref, gq_ref, wq_ref,
                       gkv_ref, wk_ref, wvt_ref, cq_ref, sq_ref, ck_ref, sk_ref,
                       qn_ref, qpe_ref, kn_ref, kpe_ref, vt_ref, zt_ref):
    xn = _rms(x_ref[...], g_ref[...]).astype(BF16)
    c = _dot(xn, wc_ref[...])
    zt_ref[0] = _dot_nt(wzt_ref[...], xn).astype(BF16)

    cq = _rms(c[:, :MLA_Q_RANK], gq_ref[...]).astype(BF16)
    ckv = _rms(c[:, MLA_Q_RANK:MLA_Q_RANK + MLA_KV_RANK], gkv_ref[...]).astype(BF16)
    off = MLA_Q_RANK + MLA_KV_RANK
    kr = c[:, off:off + LANES]
    half = MLA_ROPE // 2
    lane = lax.broadcasted_iota(jnp.int32, (1, LANES), 1)
    kr_rot = jnp.where(lane % MLA_ROPE < half, -pltpu.roll(kr, LANES - half, 1), pltpu.roll(kr, half, 1))
    kpe_ref[...] = (kr * ck_ref[...] + kr_rot * sk_ref[...]).astype(BF16)

    qt = _dot_nt(wq_ref[...], cq)
    nope = MLA_HEADS * MLA_NOPE
    qn_ref[0] = qt[:nope].astype(BF16)
    cos_h, sin_h = cq_ref[:half], sq_ref[:half]
    for h in range(MLA_HEADS):
        x1 = qt[nope + h * MLA_ROPE:nope + h * MLA_ROPE + half]
        x2 = qt[nope + h * MLA_ROPE + half:nope + (h + 1) * MLA_ROPE]
        qpe_ref[0, h * MLA_ROPE:h * MLA_ROPE + half] = (x1 * cos_h - x2 * sin_h).astype(BF16)
        qpe_ref[0, h * MLA_ROPE + half:(h + 1) * MLA_ROPE] = (x2 * cos_h + x1 * sin_h).astype(BF16)

    kn_ref[...] = _dot(ckv, wk_ref[...]).astype(BF16)
    vt = _dot_nt(wvt_ref[...], ckv).astype(BF16)
    tm = vt.shape[1]
    pad_rows = lax.broadcasted_iota(jnp.int32, (MLA_V_PAD - MLA_V, MLA_BK), 0)
    ones_row = jnp.where(pad_rows == 0, 1.0, 0.0).astype(BF16)
    for h in range(MLA_HEADS):
        for s in range(tm // MLA_BK):
            vt_ref[0, h, s, :MLA_V] = vt[h * MLA_V:(h + 1) * MLA_V, s * MLA_BK:(s + 1) * MLA_BK]
            vt_ref[0, h, s, MLA_V:] = ones_row


def _mla_inproj(x2, g, wc, wzt, gq, wq, gkv, wk, wvt, cosq, sinq, cosk, sink, batch, seq):
    m = x2.shape[0]
    tm = TM_PROJ
    per_seq = seq // tm
    sub = tm // MLA_BK
    full = lambda a: pl.BlockSpec(a.shape, lambda i: (0,) * a.ndim)
    rowblk = lambda n: pl.BlockSpec((tm, n), lambda i: (i, 0))
    tabblk = lambda n: pl.BlockSpec((tm, n), lambda i: (i % per_seq, 0))
    fmtab = lambda n: pl.BlockSpec((n, tm), lambda i: (0, i % per_seq))
    fmblk = lambda n: pl.BlockSpec((1, n, tm), lambda i: (i // per_seq, 0, i % per_seq))
    nope = MLA_HEADS * MLA_NOPE
    pe = MLA_HEADS * MLA_ROPE
    return pl.pallas_call(
        _mla_inproj_kernel,
        grid=(m // tm,),
        in_specs=[rowblk(D_MODEL), full(g), full(wc), full(wzt), full(gq), full(wq),
                  full(gkv), full(wk), full(wvt),
                  fmtab(MLA_ROPE), fmtab(MLA_ROPE), tabblk(LANES), tabblk(LANES)],
        out_specs=[
            fmblk(nope), fmblk(pe), rowblk(nope), rowblk(LANES),
            pl.BlockSpec((1, MLA_HEADS, sub, MLA_V_PAD, MLA_BK),
                         lambda i: (i // per_seq, 0, i % per_seq, 0, 0)),
            pl.BlockSpec((1, D_MODEL, tm), lambda i: (i // per_seq, 0, i % per_seq)),
        ],
        out_shape=[
            jax.ShapeDtypeStruct((batch, nope, seq), BF16),
            jax.ShapeDtypeStruct((batch, pe, seq), BF16),
            jax.ShapeDtypeStruct((m, nope), BF16),
            jax.ShapeDtypeStruct((m, LANES), BF16),
            jax.ShapeDtypeStruct((batch, MLA_HEADS, seq // MLA_BK, MLA_V_PAD, MLA_BK), BF16),
            jax.ShapeDtypeStruct((batch, D_MODEL, seq), BF16),
        ],
        compiler_params=pltpu.CompilerParams(
            dimension_semantics=("arbitrary",),
            vmem_limit_bytes=VMEM_LIMIT),
        name="mla_inproj",
    )(x2, g, wc, wzt, gq, wq, gkv, wk, wvt, cosq, sinq, cosk, sink)


def _mla_attn_rescaling(q_scr, kn_ref, kpe_ref, vt_ref, ot_ref):
    nkb = kn_ref.shape[0] // MLA_BK
    for qb in range(MLA_QSTEPS):
        cols = slice(qb * MLA_BQ, (qb + 1) * MLA_BQ)

        def body(kb, carry, cols=cols):
            m, acc = carry
            off = pl.multiple_of(kb * MLA_BK, MLA_BK)
            k = jnp.concatenate([kn_ref[pl.ds(off, MLA_BK), :], kpe_ref[pl.ds(off, MLA_BK), :]], axis=1)
            st = _dot(k, q_scr[:, cols])
            m_new = jnp.maximum(m, jnp.max(st, axis=0, keepdims=True))
            p = jnp.exp2(st - m_new)
            acc = jnp.exp2(m - m_new) * acc + _dot(vt_ref[0, 0, kb], p.astype(BF16))
            return m_new, acc

        init = (jnp.full((1, MLA_BQ), -jnp.inf, F32), jnp.zeros((MLA_V_PAD, MLA_BQ), F32))
        _, acc = lax.fori_loop(0, nkb, body, init)
        ot_ref[0, :, cols] = (acc[:MLA_V] / acc[MLA_V:MLA_V + 1]).astype(BF16)


def _mla_attn_kernel(zero_ref, qn_ref, qpe_ref, kn_ref, kpe_ref, vt_ref, ot_ref, q_scr, s0_scr, s1_scr):
    h = pl.program_id(1)
    row = lax.broadcasted_iota(jnp.int32, (LANES, 1), 0)
    qn = qn_ref[0]
    qp = qpe_ref[0]
    q_scr[:LANES] = jnp.where(row // MLA_NOPE == h % (LANES // MLA_NOPE), qn, jnp.zeros_like(qn))
    q_scr[LANES:] = jnp.where(row // MLA_ROPE == h % (LANES // MLA_ROPE), qp, jnp.zeros_like(qp))
    nkb = kn_ref.shape[0] // MLA_BK

    z = zero_ref[0]
    s_bufs = (s0_scr, s1_scr)
    blocks = [(qb, kb) for qb in range(MLA_QSTEPS) for kb in range(nkb)]

    def scores(i):
        qb, kb = blocks[i]
        rows = slice(kb * MLA_BK, (kb + 1) * MLA_BK)
        k = jnp.concatenate([kn_ref[rows, :], kpe_ref[rows, :]], axis=1)
        s_bufs[i % 2][z] = _dot(k, q_scr[:, qb * MLA_BQ:(qb + 1) * MLA_BQ])

    finite = None
    scores(0)
    for i, (qb, kb) in enumerate(blocks):
        if i + 1 < len(blocks):
            scores(i + 1)
        if kb == 0:
            m_ref = jnp.max(s_bufs[i % 2][z], axis=0, keepdims=True)
            acc = jnp.zeros((MLA_V_PAD, MLA_BQ), F32)
        acc = acc + _dot(vt_ref[0, 0, kb], jnp.exp2(s_bufs[i % 2][z] - m_ref).astype(BF16))
        if kb == nkb - 1:
            denom = acc[MLA_V:MLA_V + 1]
            ot_ref[0, :, qb * MLA_BQ:(qb + 1) * MLA_BQ] = (acc[:MLA_V] / denom).astype(BF16)
            ok = jnp.max(denom) < MLA_DENOM_LIMIT
            finite = ok if finite is None else jnp.logical_and(finite, ok)

    @pl.when(jnp.logical_not(finite))
    def _():
        _mla_attn_rescaling(q_scr, kn_ref, kpe_ref, vt_ref, ot_ref)


def _mla_attn(qn, qpe, kn, kpe, vt, batch, seq):
    bq = MLA_QSTEPS * MLA_BQ
    per_n = LANES // MLA_NOPE
    per_r = LANES // MLA_ROPE
    return pl.pallas_call(
        _mla_attn_kernel,
        grid=(batch, MLA_HEADS, seq // bq),
        in_specs=[
            pl.BlockSpec(memory_space=pltpu.SMEM),
            pl.BlockSpec((1, LANES, bq), lambda b, h, i: (b, h // per_n, i)),
            pl.BlockSpec((1, LANES, bq), lambda b, h, i: (b, h // per_r, i)),
            pl.BlockSpec((seq, LANES), lambda b, h, i: (b, h // per_n)),
            pl.BlockSpec((seq, LANES), lambda b, h, i: (b, 0)),
            pl.BlockSpec((1, 1, seq // MLA_BK, MLA_V_PAD, MLA_BK), lambda b, h, i: (b, h, 0, 0, 0)),
        ],
        out_specs=pl.BlockSpec((1, MLA_V, bq), lambda b, h, i: (b, h, i)),
        out_shape=jax.ShapeDtypeStruct((batch, MLA_HEADS * MLA_V, seq), BF16),
        scratch_shapes=[pltpu.VMEM((2 * LANES, bq), BF16),
                        pltpu.VMEM((2, MLA_BK, MLA_BQ), F32),
                        pltpu.VMEM((2, MLA_BK, MLA_BQ), F32)],
        compiler_params=pltpu.CompilerParams(
            dimension_semantics=("arbitrary", "arbitrary", "arbitrary"),
            vmem_limit_bytes=VMEM_LIMIT),
        name="mla_attn",
    )(jnp.zeros((1,), jnp.int32), qn, qpe, kn, kpe, vt)


def _rope_tables(seq):
    inv = 1.0 / (ROPE_BASE ** (jnp.arange(0, MLA_ROPE, 2, dtype=F32) / MLA_ROPE))
    ang = jnp.arange(seq, dtype=F32)[:, None] * inv[None, :]
    cos = jnp.concatenate([jnp.cos(ang), jnp.cos(ang)], axis=-1)
    sin = jnp.concatenate([jnp.sin(ang), jnp.sin(ang)], axis=-1)
    return cos, sin


def kernel(x, p, norm_g, na_w_in, na_rpb, na_w_out, mla_w_in, mla_q_norm, mla_w_qb,
           mla_kv_norm, mla_w_kvb, mla_w_out, ple_norm, ple_w_gate, ple_w_proj, final_norm):
    batch, seq, d = x.shape
    m = batch * seq
    x2 = x.reshape(m, d)
    p3 = p.reshape(p.shape[0], m, PLE_DIM)
    row = lambda v: v.reshape(1, -1).astype(F32)

    hd = NA_HEADS * NA_HEAD_DIM
    q_scale = jnp.where(jnp.arange(4 * hd) < hd, NA_HEAD_DIM ** -0.5 * LOG2E, 1.0).astype(F32)
    w_in0 = (na_w_in.reshape(d, 4 * hd) * q_scale).astype(BF16)
    qkvz = _na_inproj(x2, row(norm_g[0]), w_in0)
    rpb_rows = jnp.pad(na_rpb.reshape(NA_HEADS * NA_ROW_OFFS, NA_COL_OFFS).astype(F32),
                       ((0, 0), (0, LANES - NA_COL_OFFS)), constant_values=NEG_BIG)
    o0 = _na_attn(qkvz, rpb_rows, batch, seq)
    x2 = _post_block(x2, o0, qkvz, 3, na_w_out[0].astype(BF16), row(ple_norm[0]),
                     ple_w_gate[0].astype(BF16), p3, 0,
                     ple_w_proj[0].astype(BF16), row(final_norm),
                     feature_major=False, final_norm=False, seq=seq)

    w_in1 = mla_w_in[0]
    o2 = MLA_Q_RANK + MLA_KV_RANK
    o3 = o2 + MLA_ROPE
    w_kr = w_in1[:, o2:o3]
    rep = LANES // MLA_ROPE
    wc = jnp.concatenate([w_in1[:, :o2], jnp.tile(w_kr, (1, rep))], axis=1).astype(BF16)
    wzt = w_in1[:, o3:].T.astype(BF16)

    scale = (MLA_NOPE + MLA_ROPE) ** -0.5 * LOG2E
    wq3 = mla_w_qb[0].reshape(MLA_Q_RANK, MLA_HEADS, MLA_NOPE + MLA_ROPE) * scale
    wq_n = wq3[:, :, :MLA_NOPE].reshape(MLA_Q_RANK, -1)
    wq_p = wq3[:, :, MLA_NOPE:]
    wq = jnp.concatenate([wq_n, wq_p.reshape(MLA_Q_RANK, -1)], axis=1).T.astype(BF16)

    wkv3 = mla_w_kvb[0].reshape(MLA_KV_RANK, MLA_HEADS, MLA_NOPE + MLA_V)
    wk = wkv3[:, :, :MLA_NOPE].reshape(MLA_KV_RANK, -1).astype(BF16)
    wvt = wkv3[:, :, MLA_NOPE:].reshape(MLA_KV_RANK, -1).T.astype(BF16)

    cos, sin = _rope_tables(seq)
    cosq, sinq = cos.T, sin.T
    cosk, sink = jnp.tile(cos, (1, rep)), jnp.tile(sin, (1, rep))

    qn, qpe, kn, kpe, vt, zt = _mla_inproj(
        x2, row(norm_g[1]), wc, wzt, row(mla_q_norm[0]), wq, row(mla_kv_norm[0]),
        wk, wvt, cosq, sinq, cosk, sink, batch, seq)
    ot = _mla_attn(qn, qpe, kn, kpe, vt, batch, seq)
    out = _post_block(x2, ot, zt, 0, mla_w_out[0].astype(BF16), row(ple_norm[1]),
                      ple_w_gate[1].astype(BF16), p3, 1,
                      ple_w_proj[1].astype(BF16), row(final_norm),
                      feature_major=True, final_norm=True, seq=seq)
    return out.reshape(batch, seq, d)
```

```python
import functools

import jax
import jax.numpy as jnp
from jax import lax
from jax.experimental import pallas as pl
from jax.experimental.pallas import tpu as pltpu

D_MODEL = 1024
GRID_W = 64
NA_WIN_ROWS = 8
NA_WIN_COLS = 16
NA_HEADS = 16
NA_HEAD_DIM = 64
MLA_HEADS = 16
MLA_Q_RANK = 384
MLA_KV_RANK = 256
MLA_NOPE = 64
MLA_ROPE = 32
MLA_V = 64
MLA_V_PAD = 128
ROPE_BASE = 10000.0
PLE_DIM = 256
EPS = 1e-6

LANES = 128
NEG_BIG = -1e30
VMEM_LIMIT = 56 * 1024 * 1024

TM_PROJ = 1024
MLA_BQ = 512
MLA_QSTEPS = 8
MLA_BK = 512
MLA_DENOM_LIMIT = 3e38
LOG2E = 1.4426950408889634

BF16 = jnp.bfloat16
F32 = jnp.float32


def _rms(x, g):
    ms = jnp.mean(x * x, axis=-1, keepdims=True)
    return x * lax.rsqrt(ms + EPS) * g


def _sigmoid(x):
    return 0.5 * jnp.tanh(0.5 * x) + 0.5


def _dot(a, b):
    return jnp.dot(a, b, preferred_element_type=F32)


def _dot_nt(a, b):
    return lax.dot_general(a, b, (((1,), (1,)), ((), ())), preferred_element_type=F32)


def _dot_tn(a, b):
    return lax.dot_general(a, b, (((0,), (0,)), ((), ())), preferred_element_type=F32)


def _na_inproj_kernel(x_ref, g_ref, w_ref, o_ref, xn_ref):
    @pl.when(pl.program_id(1) == 0)
    def _():
        xn_ref[...] = _rms(x_ref[...], g_ref[...]).astype(BF16)

    o_ref[...] = _dot(xn_ref[...], w_ref[...]).astype(BF16)


def _na_inproj(x2, g, w):
    m = x2.shape[0]
    n = w.shape[1]
    tn = n
    return pl.pallas_call(
        _na_inproj_kernel,
        grid=(m // TM_PROJ, n // tn),
        in_specs=[
            pl.BlockSpec((TM_PROJ, D_MODEL), lambda i, j: (i, 0)),
            pl.BlockSpec((1, D_MODEL), lambda i, j: (0, 0)),
            pl.BlockSpec((D_MODEL, tn), lambda i, j: (0, j)),
        ],
        out_specs=pl.BlockSpec((TM_PROJ, tn), lambda i, j: (i, j)),
        out_shape=jax.ShapeDtypeStruct((m, n), BF16),
        scratch_shapes=[pltpu.VMEM((TM_PROJ, D_MODEL), BF16)],
        compiler_params=pltpu.CompilerParams(
            dimension_semantics=("arbitrary", "arbitrary"),
            vmem_limit_bytes=VMEM_LIMIT),
        name="na_inproj",
    )(x2, g, w)


NA_BAND = NA_WIN_ROWS * GRID_W
NA_SLABS = NA_HEADS * NA_HEAD_DIM // LANES
NA_ROW_OFFS = 2 * NA_WIN_ROWS - 1
NA_COL_OFFS = 2 * NA_WIN_COLS - 1
NA_PAIR_OFFS = NA_ROW_OFFS - 1
NA_ROWS_PER_STEP = 8


def _na_build_bias(rpb_ref, bias_scr):
    c = lax.broadcasted_iota(jnp.int32, (GRID_W, LANES), 0)
    lane = lax.broadcasted_iota(jnp.int32, (GRID_W, LANES), 1)
    j = lane % GRID_W
    cs = jnp.clip(c - NA_WIN_COLS // 2, 0, GRID_W - NA_WIN_COLS)
    valid = (j >= cs) & (j < cs + NA_WIN_COLS)
    diag = jnp.where(valid, j - c + (NA_WIN_COLS - 1), -1)
    second = lax.broadcasted_iota(jnp.int32, (1, LANES), 1) >= GRID_W

    def tile(idx, carry):
        h = idx // NA_PAIR_OFFS
        o = idx % NA_PAIR_OFFS
        base = (h * NA_ROW_OFFS + o) * NA_COL_OFFS
        accs = [jnp.full((GRID_W, LANES), NEG_BIG, F32) for _ in range(2)]
        for k in range(NA_COL_OFFS):
            val = jnp.where(second, rpb_ref[base + NA_COL_OFFS + k], rpb_ref[base + k]) * LOG2E
            accs[k % 2] = jnp.where(diag == k, val, accs[k % 2])
        bias_scr[h, o] = jnp.maximum(accs[0], accs[1])
        return carry

    lax.fori_loop(0, NA_HEADS * NA_PAIR_OFFS, tile, 0, unroll=2)


def _na_attn_kernel(zero_ref, rpb_ref, q_ref, k_ref, v_ref, o_ref, bias_scr, s0_scr, s1_scr, p0_scr, p1_scr):
    first_row = pl.program_id(1) * NA_ROWS_PER_STEP

    @pl.when((pl.program_id(0) == 0) & (first_row == 0))
    def _():
        _na_build_bias(rpb_ref, bias_scr)

    z = zero_ref[0]
    s_bufs = (s0_scr, s1_scr)
    p_bufs = (p0_scr, p1_scr)
    rows = k_ref.shape[0] // GRID_W
    first = lax.broadcasted_iota(jnp.int32, (1, LANES), 1) < NA_HEAD_DIM
    starts, offs = [], []
    for rr in range(NA_ROWS_PER_STEP):
        r = first_row + rr
        rs = jnp.clip(r - NA_WIN_ROWS // 2, 0, rows - NA_WIN_ROWS)
        starts.append(pl.multiple_of(rs * GRID_W, GRID_W))
        offs.append((NA_WIN_ROWS - 1) - (r - rs))
    work = [(rr, slab) for rr in range(NA_ROWS_PER_STEP) for slab in range(NA_SLABS)]

    def scores(i):
        rr, slab = work[i]
        cols = slice(slab * LANES, (slab + 1) * LANES)
        qs = q_ref[rr * GRID_W:(rr + 1) * GRID_W, cols]
        zero = jnp.zeros_like(qs)
        q2 = jnp.concatenate([jnp.where(first, qs, zero), jnp.where(first, zero, qs)], axis=0)
        s_bufs[i % 2][z] = _dot_nt(q2, k_ref[pl.ds(starts[rr], NA_BAND), cols])

    def softmax(i):
        rr, slab = work[i]
        bias = jnp.concatenate(
            [jnp.concatenate([bias_scr[2 * slab + e, offs[rr] + 2 * a] for a in range(NA_WIN_ROWS // 2)], axis=1)
             for e in range(2)], axis=0)
        s = s_bufs[i % 2][z] + bias
        m = jnp.max(s, axis=-1, keepdims=True)
        p_bufs[i % 2][z] = jnp.exp2(s - m).astype(BF16)

    ones = jnp.ones((NA_BAND, LANES), BF16)

    def values(i):
        rr, slab = work[i]
        cols = slice(slab * LANES, (slab + 1) * LANES)
        v1 = jnp.concatenate([v_ref[pl.ds(starts[rr], NA_BAND), cols], ones], axis=1)
        pv = _dot(p_bufs[i % 2][z], v1)
        pv = pv[:, :LANES] / pv[:, LANES:]
        o_ref[rr * GRID_W:(rr + 1) * GRID_W, cols] = jnp.where(first, pv[:GRID_W], pv[GRID_W:]).astype(BF16)

    n = len(work)
    scores(0)
    scores(1)
    softmax(0)
    for i in range(n):
        if i + 2 < n:
            scores(i + 2)
        if i + 1 < n:
            softmax(i + 1)
        values(i)


def _na_attn(qkvz, rpb_flat, batch, seq):
    hd = NA_HEADS * NA_HEAD_DIM
    steps = seq // GRID_W // NA_ROWS_PER_STEP
    tq = NA_ROWS_PER_STEP * GRID_W
    return pl.pallas_call(
        _na_attn_kernel,
        grid=(batch, steps),
        in_specs=[
            pl.BlockSpec(memory_space=pltpu.SMEM),
            pl.BlockSpec(memory_space=pltpu.SMEM),
            pl.BlockSpec((tq, hd), lambda b, r: (b * steps + r, 0)),
            pl.BlockSpec((seq, hd), lambda b, r: (b, 1)),
            pl.BlockSpec((seq, hd), lambda b, r: (b, 2)),
        ],
        out_specs=pl.BlockSpec((tq, hd), lambda b, r: (b * steps + r, 0)),
        out_shape=jax.ShapeDtypeStruct((batch * seq, hd), BF16),
        scratch_shapes=[pltpu.VMEM((NA_HEADS, NA_PAIR_OFFS, GRID_W, LANES), F32),
                        pltpu.VMEM((2, 2 * GRID_W, NA_BAND), F32),
                        pltpu.VMEM((2, 2 * GRID_W, NA_BAND), F32),
                        pltpu.VMEM((2, 2 * GRID_W, NA_BAND), BF16),
                        pltpu.VMEM((2, 2 * GRID_W, NA_BAND), BF16)],
        compiler_params=pltpu.CompilerParams(
            dimension_semantics=("arbitrary", "arbitrary"),
            vmem_limit_bytes=VMEM_LIMIT),
        name="na_attn",
    )(jnp.zeros((1,), jnp.int32), rpb_flat, qkvz, qkvz, qkvz)


def _post_kernel(x_ref, o_ref, z_ref, wo_ref, gp_ref, wg_ref, p_ref, wp_ref, gf_ref,
                 out_ref, *, feature_major, final_norm):
    if feature_major:
        z = z_ref[0].astype(F32)
        gated = (o_ref[0].astype(F32) * (z * _sigmoid(z))).astype(BF16)
        y = _dot_tn(gated, wo_ref[...])
    else:
        z = z_ref[...].astype(F32)
        gated = (o_ref[...].astype(F32) * (z * _sigmoid(z))).astype(BF16)
        y = _dot(gated, wo_ref[...])
    h = x_ref[...] + y
    hn = _rms(h, gp_ref[...]).astype(BF16)
    gate = _sigmoid(_dot(hn, wg_ref[...]))
    emb = _dot(p_ref[0].astype(BF16), wp_ref[...])
    xo = h + gate * emb
    if final_norm:
        xo = _rms(xo, gf_ref[...])
    out_ref[...] = xo


def _post_block(x2, o, z, z_col, w_out, g_ple, w_gate, p3, layer, w_proj, g_final, *,
                feature_major, final_norm, seq):
    m = x2.shape[0]
    tm = TM_PROJ
    per_seq = seq // tm
    if feature_major:
        oz_block = (1, D_MODEL, tm)
        o_spec = pl.BlockSpec(oz_block, lambda i: (i // per_seq, 0, i % per_seq))
        z_spec = pl.BlockSpec(oz_block, lambda i: (i // per_seq, 0, i % per_seq))
    else:
        o_spec = pl.BlockSpec((tm, D_MODEL), lambda i: (i, 0))
        z_spec = pl.BlockSpec((tm, D_MODEL), lambda i: (i, z_col))
    full = lambda shape: pl.BlockSpec(shape, lambda i: (0,) * len(shape))
    kern = functools.partial(_post_kernel, feature_major=feature_major, final_norm=final_norm)
    return pl.pallas_call(
        kern,
        grid=(m // tm,),
        in_specs=[
            pl.BlockSpec((tm, D_MODEL), lambda i: (i, 0)),
            o_spec,
            z_spec,
            full((D_MODEL, D_MODEL)),
            full((1, D_MODEL)),
            full((D_MODEL, D_MODEL)),
            pl.BlockSpec((1, tm, PLE_DIM), lambda i: (layer, i, 0)),
            full((PLE_DIM, D_MODEL)),
            full((1, D_MODEL)),
        ],
        out_specs=pl.BlockSpec((tm, D_MODEL), lambda i: (i, 0)),
        out_shape=jax.ShapeDtypeStruct((m, D_MODEL), F32),
        compiler_params=pltpu.CompilerParams(
            dimension_semantics=("arbitrary",),
            vmem_limit_bytes=VMEM_LIMIT),
        name="post_block_fm" if feature_major else "post_block",
    )(x2, o, z, w_out, g_ple, w_gate, p3, w_proj, g_final)


def _mla_inproj_kernel(x_ref, g_ref, wc_ref, wzt_ref, gq_ref, wq_ref,
                       gkv_ref, wk_ref, wvt_ref, cq_ref, sq_ref, ck_ref, sk_ref,
                       qn_ref, qpe_ref, kn_ref, kpe_ref, vt_ref, zt_ref):
    xn = _rms(x_ref[...], g_ref[...]).astype(BF16)
    c = _dot(xn, wc_ref[...])
    zt_ref[0] = _dot_nt(wzt_ref[...], xn).astype(BF16)

    cq = _rms(c[:, :MLA_Q_RANK], gq_ref[...]).astype(BF16)
    ckv = _rms(c[:, MLA_Q_RANK:MLA_Q_RANK + MLA_KV_RANK], gkv_ref[...]).astype(BF16)
    off = MLA_Q_RANK + MLA_KV_RANK
    kr = c[:, off:off + LANES]
    half = MLA_ROPE // 2
    lane = lax.broadcasted_iota(jnp.int32, (1, LANES), 1)
    kr_rot = jnp.where(lane % MLA_ROPE < half, -pltpu.roll(kr, LANES - half, 1), pltpu.roll(kr, half, 1))
    kpe_ref[...] = (kr * ck_ref[...] + kr_rot * sk_ref[...]).astype(BF16)

    qt = _dot_nt(wq_ref[...], cq)
    nope = MLA_HEADS * MLA_NOPE
    qn_ref[0] = qt[:nope].astype(BF16)
    cos_h, sin_h = cq_ref[:half], sq_ref[:half]
    for h in range(MLA_HEADS):
        x1 = qt[nope + h * MLA_ROPE:nope + h * MLA_ROPE + half]
        x2 = qt[nope + h * MLA_ROPE + half:nope + (h + 1) * MLA_ROPE]
        qpe_ref[0, h * MLA_ROPE:h * MLA_ROPE + half] = (x1 * cos_h - x2 * sin_h).astype(BF16)
        qpe_ref[0, h * MLA_ROPE + half:(h + 1) * MLA_ROPE] = (x2 * cos_h + x1 * sin_h).astype(BF16)

    kn_ref[...] = _dot(ckv, wk_ref[...]).astype(BF16)
    vt = _dot_nt(wvt_ref[...], ckv).astype(BF16)
    tm = vt.shape[1]
    pad_rows = lax.broadcasted_iota(jnp.int32, (MLA_V_PAD - MLA_V, MLA_BK), 0)
    ones_row = jnp.where(pad_rows == 0, 1.0, 0.0).astype(BF16)
    for h in range(MLA_HEADS):
        for s in range(tm // MLA_BK):
            vt_ref[0, h, s, :MLA_V] = vt[h * MLA_V:(h + 1) * MLA_V, s * MLA_BK:(s + 1) * MLA_BK]
            vt_ref[0, h, s, MLA_V:] = ones_row


def _mla_inproj(x2, g, wc, wzt, gq, wq, gkv, wk, wvt, cosq, sinq, cosk, sink, batch, seq):
    m = x2.shape[0]
    tm = TM_PROJ
    per_seq = seq // tm
    sub = tm // MLA_BK
    full = lambda a: pl.BlockSpec(a.shape, lambda i: (0,) * a.ndim)
    rowblk = lambda n: pl.BlockSpec((tm, n), lambda i: (i, 0))
    tabblk = lambda n: pl.BlockSpec((tm, n), lambda i: (i % per_seq, 0))
    fmtab = lambda n: pl.BlockSpec((n, tm), lambda i: (0, i % per_seq))
    fmblk = lambda n: pl.BlockSpec((1, n, tm), lambda i: (i // per_seq, 0, i % per_seq))
    nope = MLA_HEADS * MLA_NOPE
    pe = MLA_HEADS * MLA_ROPE
    return pl.pallas_call(
        _mla_inproj_kernel,
        grid=(m // tm,),
        in_specs=[rowblk(D_MODEL), full(g), full(wc), full(wzt), full(gq), full(wq),
                  full(gkv), full(wk), full(wvt),
                  fmtab(MLA_ROPE), fmtab(MLA_ROPE), tabblk(LANES), tabblk(LANES)],
        out_specs=[
            fmblk(nope), fmblk(pe), rowblk(nope), rowblk(LANES),
            pl.BlockSpec((1, MLA_HEADS, sub, MLA_V_PAD, MLA_BK),
                         lambda i: (i // per_seq, 0, i % per_seq, 0, 0)),
            pl.BlockSpec((1, D_MODEL, tm), lambda i: (i // per_seq, 0, i % per_seq)),
        ],
        out_shape=[
            jax.ShapeDtypeStruct((batch, nope, seq), BF16),
            jax.ShapeDtypeStruct((batch, pe, seq), BF16),
            jax.ShapeDtypeStruct((m, nope), BF16),
            jax.ShapeDtypeStruct((m, LANES), BF16),
            jax.ShapeDtypeStruct((batch, MLA_HEADS, seq // MLA_BK, MLA_V_PAD, MLA_BK), BF16),
            jax.ShapeDtypeStruct((batch, D_MODEL, seq), BF16),
        ],
        compiler_params=pltpu.CompilerParams(
            dimension_semantics=("arbitrary",),
            vmem_limit_bytes=VMEM_LIMIT),
        name="mla_inproj",
    )(x2, g, wc, wzt, gq, wq, gkv, wk, wvt, cosq, sinq, cosk, sink)


def _mla_attn_rescaling(q_scr, kn_ref, kpe_ref, vt_ref, ot_ref):
    nkb = kn_ref.shape[0] // MLA_BK
    for qb in range(MLA_QSTEPS):
        cols = slice(qb * MLA_BQ, (qb + 1) * MLA_BQ)

        def body(kb, carry, cols=cols):
            m, acc = carry
            off = pl.multiple_of(kb * MLA_BK, MLA_BK)
            k = jnp.concatenate([kn_ref[pl.ds(off, MLA_BK), :], kpe_ref[pl.ds(off, MLA_BK), :]], axis=1)
            st = _dot(k, q_scr[:, cols])
            m_new = jnp.maximum(m, jnp.max(st, axis=0, keepdims=True))
            p = jnp.exp2(st - m_new)
            acc = jnp.exp2(m - m_new) * acc + _dot(vt_ref[0, 0, kb], p.astype(BF16))
            return m_new, acc

        init = (jnp.full((1, MLA_BQ), -jnp.inf, F32), jnp.zeros((MLA_V_PAD, MLA_BQ), F32))
        _, acc = lax.fori_loop(0, nkb, body, init)
        ot_ref[0, :, cols] = (acc[:MLA_V] / acc[MLA_V:MLA_V + 1]).astype(BF16)


def _mla_attn_kernel(zero_ref, qn_ref, qpe_ref, kn_ref, kpe_ref, vt_ref, ot_ref, q_scr, s0_scr, s1_scr):
    h = pl.program_id(1)
    row = lax.broadcasted_iota(jnp.int32, (LANES, 1), 0)
    qn = qn_ref[0]
    qp = qpe_ref[0]
    q_scr[:LANES] = jnp.where(row // MLA_NOPE == h % (LANES // MLA_NOPE), qn, jnp.zeros_like(qn))
    q_scr[LANES:] = jnp.where(row // MLA_ROPE == h % (LANES // MLA_ROPE), qp, jnp.zeros_like(qp))
    nkb = kn_ref.shape[0] // MLA_BK

    z = zero_ref[0]
    s_bufs = (s0_scr, s1_scr)
    blocks = [(qb, kb) for qb in range(MLA_QSTEPS) for kb in range(nkb)]

    def scores(i):
        qb, kb = blocks[i]
        rows = slice(kb * MLA_BK, (kb + 1) * MLA_BK)
        k = jnp.concatenate([kn_ref[rows, :], kpe_ref[rows, :]], axis=1)
        s_bufs[i % 2][z] = _dot(k, q_scr[:, qb * MLA_BQ:(qb + 1) * MLA_BQ])

    finite = None
    scores(0)
    for i, (qb, kb) in enumerate(blocks):
        if i + 1 < len(blocks):
            scores(i + 1)
        if kb == 0:
            m_ref = jnp.max(s_bufs[i % 2][z], axis=0, keepdims=True)
            acc = jnp.zeros((MLA_V_PAD, MLA_BQ), F32)
        acc = acc + _dot(vt_ref[0, 0, kb], jnp.exp2(s_bufs[i % 2][z] - m_ref).astype(BF16))
        if kb == nkb - 1:
            denom = acc[MLA_V:MLA_V + 1]
            ot_ref[0, :, qb * MLA_BQ:(qb + 1) * MLA_BQ] = (acc[:MLA_V] / denom).astype(BF16)
            ok = jnp.max(denom) < MLA_DENOM_LIMIT
            finite = ok if finite is None else jnp.logical_and(finite, ok)

    @pl.when(jnp.logical_not(finite))
    def _():
        _mla_attn_rescaling(q_scr, kn_ref, kpe_ref, vt_ref, ot_ref)


def _mla_attn(qn, qpe, kn, kpe, vt, batch, seq):
    bq = MLA_QSTEPS * MLA_BQ
    per_n = LANES // MLA_NOPE
    per_r = LANES // MLA_ROPE
    return pl.pallas_call(
        _mla_attn_kernel,
        grid=(batch, MLA_HEADS, seq // bq),
        in_specs=[
            pl.BlockSpec(memory_space=pltpu.SMEM),
            pl.BlockSpec((1, LANES, bq), lambda b, h, i: (b, h // per_n, i)),
            pl.BlockSpec((1, LANES, bq), lambda b, h, i: (b, h // per_r, i)),
            pl.BlockSpec((seq, LANES), lambda b, h, i: (b, h // per_n)),
            pl.BlockSpec((seq, LANES), lambda b, h, i: (b, 0)),
            pl.BlockSpec((1, 1, seq // MLA_BK, MLA_V_PAD, MLA_BK), lambda b, h, i: (b, h, 0, 0, 0)),
        ],
        out_specs=pl.BlockSpec((1, MLA_V, bq), lambda b, h, i: (b, h, i)),
        out_shape=jax.ShapeDtypeStruct((batch, MLA_HEADS * MLA_V, seq), BF16),
        scratch_shapes=[pltpu.VMEM((2 * LANES, bq), BF16),
                        pltpu.VMEM((2, MLA_BK, MLA_BQ), F32),
                        pltpu.VMEM((2, MLA_BK, MLA_BQ), F32)],
        compiler_params=pltpu.CompilerParams(
            dimension_semantics=("arbitrary", "arbitrary", "arbitrary"),
            vmem_limit_bytes=VMEM_LIMIT),
        name="mla_attn",
    )(jnp.zeros((1,), jnp.int32), qn, qpe, kn, kpe, vt)


def _rope_tables(seq):
    inv = 1.0 / (ROPE_BASE ** (jnp.arange(0, MLA_ROPE, 2, dtype=F32) / MLA_ROPE))
    ang = jnp.arange(seq, dtype=F32)[:, None] * inv[None, :]
    cos = jnp.concatenate([jnp.cos(ang), jnp.cos(ang)], axis=-1)
    sin = jnp.concatenate([jnp.sin(ang), jnp.sin(ang)], axis=-1)
    return cos, sin


def kernel(x, p, norm_g, na_w_in, na_rpb, na_w_out, mla_w_in, mla_q_norm, mla_w_qb,
           mla_kv_norm, mla_w_kvb, mla_w_out, ple_norm, ple_w_gate, ple_w_proj, final_norm):
    batch, seq, d = x.shape
    m = batch * seq
    x2 = x.reshape(m, d)
    p3 = p.reshape(p.shape[0], m, PLE_DIM)
    row = lambda v: v.reshape(1, -1).astype(F32)

    hd = NA_HEADS * NA_HEAD_DIM
    q_scale = jnp.where(jnp.arange(4 * hd) < hd, NA_HEAD_DIM ** -0.5 * LOG2E, 1.0).astype(F32)
    w_in0 = (na_w_in.reshape(d, 4 * hd) * q_scale).astype(BF16)
    qkvz = _na_inproj(x2, row(norm_g[0]), w_in0)
    o0 = _na_attn(qkvz, na_rpb.reshape(-1).astype(F32), batch, seq)
    x2 = _post_block(x2, o0, qkvz, 3, na_w_out[0].astype(BF16), row(ple_norm[0]),
                     ple_w_gate[0].astype(BF16), p3, 0,
                     ple_w_proj[0].astype(BF16), row(final_norm),
                     feature_major=False, final_norm=False, seq=seq)

    w_in1 = mla_w_in[0]
    o2 = MLA_Q_RANK + MLA_KV_RANK
    o3 = o2 + MLA_ROPE
    w_kr = w_in1[:, o2:o3]
    rep = LANES // MLA_ROPE
    wc = jnp.concatenate([w_in1[:, :o2], jnp.tile(w_kr, (1, rep))], axis=1).astype(BF16)
    wzt = w_in1[:, o3:].T.astype(BF16)

    scale = (MLA_NOPE + MLA_ROPE) ** -0.5 * LOG2E
    wq3 = mla_w_qb[0].reshape(MLA_Q_RANK, MLA_HEADS, MLA_NOPE + MLA_ROPE) * scale
    wq_n = wq3[:, :, :MLA_NOPE].reshape(MLA_Q_RANK, -1)
    wq_p = wq3[:, :, MLA_NOPE:]
    wq = jnp.concatenate([wq_n, wq_p.reshape(MLA_Q_RANK, -1)], axis=1).T.astype(BF16)

    wkv3 = mla_w_kvb[0].reshape(MLA_KV_RANK, MLA_HEADS, MLA_NOPE + MLA_V)
    wk = wkv3[:, :, :MLA_NOPE].reshape(MLA_KV_RANK, -1).astype(BF16)
    wvt = wkv3[:, :, MLA_NOPE:].reshape(MLA_KV_RANK, -1).T.astype(BF16)

    cos, sin = _rope_tables(seq)
    cosq, sinq = cos.T, sin.T
    cosk, sink = jnp.tile(cos, (1, rep)), jnp.tile(sin, (1, rep))

    qn, qpe, kn, kpe, vt, zt = _mla_inproj(
        x2, row(norm_g[1]), wc, wzt, row(mla_q_norm[0]), wq, row(mla_kv_norm[0]),
        wk, wvt, cosq, sinq, cosk, sink, batch, seq)
    ot = _mla_attn(qn, qpe, kn, kpe, vt, batch, seq)
    out = _post_block(x2, ot, zt, 0, mla_w_out[0].astype(BF16), row(ple_norm[1]),
                      ple_w_gate[1].astype(BF16), p3, 1,
                      ple_w_proj[1].astype(BF16), row(final_norm),
                      feature_major=True, final_norm=True, seq=seq)
    return out.reshape(batch, seq, d)
```

```python
import functools

import jax
import jax.numpy as jnp
from jax import lax
from jax.experimental import pallas as pl
from jax.experimental.pallas import tpu as pltpu

D_MODEL = 1024
GRID_W = 64
NA_WIN_ROWS = 8
NA_WIN_COLS = 16
NA_HEADS = 16
NA_HEAD_DIM = 64
MLA_HEADS = 16
MLA_Q_RANK = 384
MLA_KV_RANK = 256
MLA_NOPE = 64
MLA_ROPE = 32
MLA_V = 64
MLA_V_PAD = 128
ROPE_BASE = 10000.0
PLE_DIM = 256
EPS = 1e-6

LANES = 128
NEG_BIG = -1e30
VMEM_LIMIT = 56 * 1024 * 1024

TM_PROJ = 1024
MLA_BQ = 512
MLA_QSTEPS = 8
MLA_BK = 512
MLA_DENOM_LIMIT = 3e38
LOG2E = 1.4426950408889634

BF16 = jnp.bfloat16
F32 = jnp.float32


def _rms(x, g):
    ms = jnp.mean(x * x, axis=-1, keepdims=True)
    return x * lax.rsqrt(ms + EPS) * g


def _sigmoid(x):
    return 0.5 * jnp.tanh(0.5 * x) + 0.5


def _dot(a, b):
    return jnp.dot(a, b, preferred_element_type=F32)


def _dot_nt(a, b):
    return lax.dot_general(a, b, (((1,), (1,)), ((), ())), preferred_element_type=F32)


def _dot_tn(a, b):
    return lax.dot_general(a, b, (((0,), (0,)), ((), ())), preferred_element_type=F32)


def _na_inproj_kernel(x_ref, g_ref, w_ref, o_ref, xn_ref):
    @pl.when(pl.program_id(1) == 0)
    def _():
        xn_ref[...] = _rms(x_ref[...], g_ref[...]).astype(BF16)

    o_ref[...] = _dot(xn_ref[...], w_ref[...]).astype(BF16)


def _na_inproj(x2, g, w):
    m = x2.shape[0]
    n = w.shape[1]
    tn = n
    return pl.pallas_call(
        _na_inproj_kernel,
        grid=(m // TM_PROJ, n // tn),
        in_specs=[
            pl.BlockSpec((TM_PROJ, D_MODEL), lambda i, j: (i, 0)),
            pl.BlockSpec((1, D_MODEL), lambda i, j: (0, 0)),
            pl.BlockSpec((D_MODEL, tn), lambda i, j: (0, j)),
        ],
        out_specs=pl.BlockSpec((TM_PROJ, tn), lambda i, j: (i, j)),
        out_shape=jax.ShapeDtypeStruct((m, n), BF16),
        scratch_shapes=[pltpu.VMEM((TM_PROJ, D_MODEL), BF16)],
        compiler_params=pltpu.CompilerParams(
            dimension_semantics=("arbitrary", "arbitrary"),
            vmem_limit_bytes=VMEM_LIMIT),
        name="na_inproj",
    )(x2, g, w)


NA_BAND = NA_WIN_ROWS * GRID_W
NA_SLABS = NA_HEADS * NA_HEAD_DIM // LANES
NA_ROW_OFFS = 2 * NA_WIN_ROWS - 1
NA_COL_OFFS = 2 * NA_WIN_COLS - 1
NA_PAIR_OFFS = NA_ROW_OFFS - 1
NA_ROWS_PER_STEP = 16


def _na_build_bias(rpb_ref, bias_scr):
    c = lax.broadcasted_iota(jnp.int32, (GRID_W, LANES), 0)
    lane = lax.broadcasted_iota(jnp.int32, (GRID_W, LANES), 1)
    j = lane % GRID_W
    cs = jnp.clip(c - NA_WIN_COLS // 2, 0, GRID_W - NA_WIN_COLS)
    valid = (j >= cs) & (j < cs + NA_WIN_COLS)
    diag = jnp.where(valid, j - c + (NA_WIN_COLS - 1), -1)
    second = lax.broadcasted_iota(jnp.int32, (1, LANES), 1) >= GRID_W

    def tile(idx, carry):
        h = idx // NA_PAIR_OFFS
        o = idx % NA_PAIR_OFFS
        base = (h * NA_ROW_OFFS + o) * NA_COL_OFFS
        accs = [jnp.full((GRID_W, LANES), NEG_BIG, F32) for _ in range(2)]
        for k in range(NA_COL_OFFS):
            val = jnp.where(second, rpb_ref[base + NA_COL_OFFS + k], rpb_ref[base + k]) * LOG2E
            accs[k % 2] = jnp.where(diag == k, val, accs[k % 2])
        bias_scr[h, o] = jnp.maximum(accs[0], accs[1])
        return carry

    lax.fori_loop(0, NA_HEADS * NA_PAIR_OFFS, tile, 0, unroll=2)


def _na_attn_kernel(zero_ref, rpb_ref, q_ref, k_ref, v_ref, o_ref, bias_scr, s0_scr, s1_scr, p0_scr, p1_scr):
    first_row = pl.program_id(1) * NA_ROWS_PER_STEP

    @pl.when((pl.program_id(0) == 0) & (first_row == 0))
    def _():
        _na_build_bias(rpb_ref, bias_scr)

    z = zero_ref[0]
    s_bufs = (s0_scr, s1_scr)
    p_bufs = (p0_scr, p1_scr)
    rows = k_ref.shape[0] // GRID_W
    first = lax.broadcasted_iota(jnp.int32, (1, LANES), 1) < NA_HEAD_DIM
    starts, offs = [], []
    for rr in range(NA_ROWS_PER_STEP):
        r = first_row + rr
        rs = jnp.clip(r - NA_WIN_ROWS // 2, 0, rows - NA_WIN_ROWS)
        starts.append(pl.multiple_of(rs * GRID_W, GRID_W))
        offs.append((NA_WIN_ROWS - 1) - (r - rs))
    work = [(rr, slab) for rr in range(NA_ROWS_PER_STEP) for slab in range(NA_SLABS)]

    def scores(i):
        rr, slab = work[i]
        cols = slice(slab * LANES, (slab + 1) * LANES)
        qs = q_ref[rr * GRID_W:(rr + 1) * GRID_W, cols]
        zero = jnp.zeros_like(qs)
        q2 = jnp.concatenate([jnp.where(first, qs, zero), jnp.where(first, zero, qs)], axis=0)
        s_bufs[i % 2][z] = _dot_nt(q2, k_ref[pl.ds(starts[rr], NA_BAND), cols])

    def softmax(i):
        rr, slab = work[i]
        bias = jnp.concatenate(
            [jnp.concatenate([bias_scr[2 * slab + e, offs[rr] + 2 * a] for a in range(NA_WIN_ROWS // 2)], axis=1)
             for e in range(2)], axis=0)
        s = s_bufs[i % 2][z] + bias
        m = jnp.max(s, axis=-1, keepdims=True)
        p_bufs[i % 2][z] = jnp.exp2(s - m).astype(BF16)

    ones = jnp.ones((NA_BAND, LANES), BF16)

    def values(i):
        rr, slab = work[i]
        cols = slice(slab * LANES, (slab + 1) * LANES)
        v1 = jnp.concatenate([v_ref[pl.ds(starts[rr], NA_BAND), cols], ones], axis=1)
        pv = _dot(p_bufs[i % 2][z], v1)
        pv = pv[:, :LANES] / pv[:, LANES:]
        o_ref[rr * GRID_W:(rr + 1) * GRID_W, cols] = jnp.where(first, pv[:GRID_W], pv[GRID_W:]).astype(BF16)

    n = len(work)
    scores(0)
    scores(1)
    softmax(0)
    for i in range(n):
        if i + 2 < n:
            scores(i + 2)
        if i + 1 < n:
            softmax(i + 1)
        values(i)


def _na_attn(qkvz, rpb_flat, batch, seq):
    hd = NA_HEADS * NA_HEAD_DIM
    steps = seq // GRID_W // NA_ROWS_PER_STEP
    tq = NA_ROWS_PER_STEP * GRID_W
    return pl.pallas_call(
        _na_attn_kernel,
        grid=(batch, steps),
        in_specs=[
            pl.BlockSpec(memory_space=pltpu.SMEM),
            pl.BlockSpec(memory_space=pltpu.SMEM),
            pl.BlockSpec((tq, hd), lambda b, r: (b * steps + r, 0)),
            pl.BlockSpec((seq, hd), lambda b, r: (b, 1)),
            pl.BlockSpec((seq, hd), lambda b, r: (b, 2)),
        ],
        out_specs=pl.BlockSpec((tq, hd), lambda b, r: (b * steps + r, 0)),
        out_shape=jax.ShapeDtypeStruct((batch * seq, hd), BF16),
        scratch_shapes=[pltpu.VMEM((NA_HEADS, NA_PAIR_OFFS, GRID_W, LANES), F32),
                        pltpu.VMEM((2, 2 * GRID_W, NA_BAND), F32),
                        pltpu.VMEM((2, 2 * GRID_W, NA_BAND), F32),
                        pltpu.VMEM((2, 2 * GRID_W, NA_BAND), BF16),
                        pltpu.VMEM((2, 2 * GRID_W, NA_BAND), BF16)],
        compiler_params=pltpu.CompilerParams(
            dimension_semantics=("arbitrary", "arbitrary"),
            vmem_limit_bytes=VMEM_LIMIT),
        name="na_attn",
    )(jnp.zeros((1,), jnp.int32), rpb_flat, qkvz, qkvz, qkvz)


def _post_kernel(x_ref, o_ref, z_ref, wo_ref, gp_ref, wg_ref, p_ref, wp_ref, gf_ref,
                 out_ref, *, feature_major, final_norm):
    if feature_major:
        z = z_ref[0].astype(F32)
        gated = (o_ref[0].astype(F32) * (z * _sigmoid(z))).astype(BF16)
        y = _dot_tn(gated, wo_ref[...])
    else:
        z = z_ref[...].astype(F32)
        gated = (o_ref[...].astype(F32) * (z * _sigmoid(z))).astype(BF16)
        y = _dot(gated, wo_ref[...])
    h = x_ref[...] + y
    hn = _rms(h, gp_ref[...]).astype(BF16)
    gate = _sigmoid(_dot(hn, wg_ref[...]))
    emb = _dot(p_ref[0].astype(BF16), wp_ref[...])
    xo = h + gate * emb
    if final_norm:
        xo = _rms(xo, gf_ref[...])
    out_ref[...] = xo


def _post_block(x2, o, z, z_col, w_out, g_ple, w_gate, p3, layer, w_proj, g_final, *,
                feature_major, final_norm, seq):
    m = x2.shape[0]
    tm = TM_PROJ
    per_seq = seq // tm
    if feature_major:
        oz_block = (1, D_MODEL, tm)
        o_spec = pl.BlockSpec(oz_block, lambda i: (i // per_seq, 0, i % per_seq))
        z_spec = pl.BlockSpec(oz_block, lambda i: (i // per_seq, 0, i % per_seq))
    else:
        o_spec = pl.BlockSpec((tm, D_MODEL), lambda i: (i, 0))
        z_spec = pl.BlockSpec((tm, D_MODEL), lambda i: (i, z_col))
    full = lambda shape: pl.BlockSpec(shape, lambda i: (0,) * len(shape))
    kern = functools.partial(_post_kernel, feature_major=feature_major, final_norm=final_norm)
    return pl.pallas_call(
        kern,
        grid=(m // tm,),
        in_specs=[
            pl.BlockSpec((tm, D_MODEL), lambda i: (i, 0)),
            o_spec,
            z_spec,
            full((D_MODEL, D_MODEL)),
            full((1, D_MODEL)),
            full((D_MODEL, D_MODEL)),
            pl.BlockSpec((1, tm, PLE_DIM), lambda i: (layer, i, 0)),
            full((PLE_DIM, D_MODEL)),
            full((1, D_MODEL)),
        ],
        out_specs=pl.BlockSpec((tm, D_MODEL), lambda i: (i, 0)),
        out_shape=jax.ShapeDtypeStruct((m, D_MODEL), F32),
        compiler_params=pltpu.CompilerParams(
            dimension_semantics=("arbitrary",),
            vmem_limit_bytes=VMEM_LIMIT),
        name="post_block_fm" if feature_major else "post_block",
    )(x2, o, z, w_out, g_ple, w_gate, p3, w_proj, g_final)


def _mla_inproj_kernel(x_ref, g_ref, wc_ref, wzt_ref, gq_ref, wq_ref,
                       gkv_ref, wk_ref, wvt_ref, cq_ref, sq_ref, ck_ref, sk_ref,
                       qn_ref, qpe_ref, kn_ref, kpe_ref, vt_ref, zt_ref):
    xn = _rms(x_ref[...], g_ref[...]).astype(BF16)
    c = _dot(xn, wc_ref[...])
    zt_ref[0] = _dot_nt(wzt_ref[...], xn).astype(BF16)

    cq = _rms(c[:, :MLA_Q_RANK], gq_ref[...]).astype(BF16)
    ckv = _rms(c[:, MLA_Q_RANK:MLA_Q_RANK + MLA_KV_RANK], gkv_ref[...]).astype(BF16)
    off = MLA_Q_RANK + MLA_KV_RANK
    kr = c[:, off:off + LANES]
    half = MLA_ROPE // 2
    lane = lax.broadcasted_iota(jnp.int32, (1, LANES), 1)
    kr_rot = jnp.where(lane % MLA_ROPE < half, -pltpu.roll(kr, LANES - half, 1), pltpu.roll(kr, half, 1))
    kpe_ref[...] = (kr * ck_ref[...] + kr_rot * sk_ref[...]).astype(BF16)

    qt = _dot_nt(wq_ref[...], cq)
    nope = MLA_HEADS * MLA_NOPE
    qn_ref[0] = qt[:nope].astype(BF16)
    cos_h, sin_h = cq_ref[:half], sq_ref[:half]
    for h in range(MLA_HEADS):
        x1 = qt[nope + h * MLA_ROPE:nope + h * MLA_ROPE + half]
        x2 = qt[nope + h * MLA_ROPE + half:nope + (h + 1) * MLA_ROPE]
        qpe_ref[0, h * MLA_ROPE:h * MLA_ROPE + half] = (x1 * cos_h - x2 * sin_h).astype(BF16)
        qpe_ref[0, h * MLA_ROPE + half:(h + 1) * MLA_ROPE] = (x2 * cos_h + x1 * sin_h).astype(BF16)

    kn_ref[...] = _dot(ckv, wk_ref[...]).astype(BF16)
    vt = _dot_nt(wvt_ref[...], ckv).astype(BF16)
    tm = vt.shape[1]
    pad_rows = lax.broadcasted_iota(jnp.int32, (MLA_V_PAD - MLA_V, MLA_BK), 0)
    ones_row = jnp.where(pad_rows == 0, 1.0, 0.0).astype(BF16)
    for h in range(MLA_HEADS):
        for s in range(tm // MLA_BK):
            vt_ref[0, h, s, :MLA_V] = vt[h * MLA_V:(h + 1) * MLA_V, s * MLA_BK:(s + 1) * MLA_BK]
            vt_ref[0, h, s, MLA_V:] = ones_row


def _mla_inproj(x2, g, wc, wzt, gq, wq, gkv, wk, wvt, cosq, sinq, cosk, sink, batch, seq):
    m = x2.shape[0]
    tm = TM_PROJ
    per_seq = seq // tm
    sub = tm // MLA_BK
    full = lambda a: pl.BlockSpec(a.shape, lambda i: (0,) * a.ndim)
    rowblk = lambda n: pl.BlockSpec((tm, n), lambda i: (i, 0))
    tabblk = lambda n: pl.BlockSpec((tm, n), lambda i: (i % per_seq, 0))
    fmtab = lambda n: pl.BlockSpec((n, tm), lambda i: (0, i % per_seq))
    fmblk = lambda n: pl.BlockSpec((1, n, tm), lambda i: (i // per_seq, 0, i % per_seq))
    nope = MLA_HEADS * MLA_NOPE
    pe = MLA_HEADS * MLA_ROPE
    return pl.pallas_call(
        _mla_inproj_kernel,
        grid=(m // tm,),
        in_specs=[rowblk(D_MODEL), full(g), full(wc), full(wzt), full(gq), full(wq),
                  full(gkv), full(wk), full(wvt),
                  fmtab(MLA_ROPE), fmtab(MLA_ROPE), tabblk(LANES), tabblk(LANES)],
        out_specs=[
            fmblk(nope), fmblk(pe), rowblk(nope), rowblk(LANES),
            pl.BlockSpec((1, MLA_HEADS, sub, MLA_V_PAD, MLA_BK),
                         lambda i: (i // per_seq, 0, i % per_seq, 0, 0)),
            pl.BlockSpec((1, D_MODEL, tm), lambda i: (i // per_seq, 0, i % per_seq)),
        ],
        out_shape=[
            jax.ShapeDtypeStruct((batch, nope, seq), BF16),
            jax.ShapeDtypeStruct((batch, pe, seq), BF16),
            jax.ShapeDtypeStruct((m, nope), BF16),
            jax.ShapeDtypeStruct((m, LANES), BF16),
            jax.ShapeDtypeStruct((batch, MLA_HEADS, seq // MLA_BK, MLA_V_PAD, MLA_BK), BF16),
            jax.ShapeDtypeStruct((batch, D_MODEL, seq), BF16),
        ],
        compiler_params=pltpu.CompilerParams(
            dimension_semantics=("arbitrary",),
            vmem_limit_bytes=VMEM_LIMIT),
        name="mla_inproj",
    )(x2, g, wc, wzt, gq, wq, gkv, wk, wvt, cosq, sinq, cosk, sink)


def _mla_attn_rescaling(q_scr, kn_ref, kpe_ref, vt_ref, ot_ref):
    nkb = kn_ref.shape[0] // MLA_BK
    for qb in range(MLA_QSTEPS):
        cols = slice(qb * MLA_BQ, (qb + 1) * MLA_BQ)

        def body(kb, carry, cols=cols):
            m, acc = carry
            off = pl.multiple_of(kb * MLA_BK, MLA_BK)
            k = jnp.concatenate([kn_ref[pl.ds(off, MLA_BK), :], kpe_ref[pl.ds(off, MLA_BK), :]], axis=1)
            st = _dot(k, q_scr[:, cols])
            m_new = jnp.maximum(m, jnp.max(st, axis=0, keepdims=True))
            p = jnp.exp2(st - m_new)
            acc = jnp.exp2(m - m_new) * acc + _dot(vt_ref[0, 0, kb], p.astype(BF16))
            return m_new, acc

        init = (jnp.full((1, MLA_BQ), -jnp.inf, F32), jnp.zeros((MLA_V_PAD, MLA_BQ), F32))
        _, acc = lax.fori_loop(0, nkb, body, init)
        ot_ref[0, :, cols] = (acc[:MLA_V] / acc[MLA_V:MLA_V + 1]).astype(BF16)


def _mla_attn_kernel(zero_ref, qn_ref, qpe_ref, kn_ref, kpe_ref, vt_ref, ot_ref, q_scr, s0_scr, s1_scr):
    h = pl.program_id(1)
    row = lax.broadcasted_iota(jnp.int32, (LANES, 1), 0)
    qn = qn_ref[0]
    qp = qpe_ref[0]
    q_scr[:LANES] = jnp.where(row // MLA_NOPE == h % (LANES // MLA_NOPE), qn, jnp.zeros_like(qn))
    q_scr[LANES:] = jnp.where(row // MLA_ROPE == h % (LANES // MLA_ROPE), qp, jnp.zeros_like(qp))
    nkb = kn_ref.shape[0] // MLA_BK

    z = zero_ref[0]
    s_bufs = (s0_scr, s1_scr)
    blocks = [(qb, kb) for qb in range(MLA_QSTEPS) for kb in range(nkb)]

    def scores(i):
        qb, kb = blocks[i]
        rows = slice(kb * MLA_BK, (kb + 1) * MLA_BK)
        k = jnp.concatenate([kn_ref[rows, :], kpe_ref[rows, :]], axis=1)
        s_bufs[i % 2][z] = _dot(k, q_scr[:, qb * MLA_BQ:(qb + 1) * MLA_BQ])

    finite = None
    scores(0)
    for i, (qb, kb) in enumerate(blocks):
        if i + 1 < len(blocks):
            scores(i + 1)
        if kb == 0:
            m_ref = jnp.max(s_bufs[i % 2][z], axis=0, keepdims=True)
            acc = jnp.zeros((MLA_V_PAD, MLA_BQ), F32)
        acc = acc + _dot(vt_ref[0, 0, kb], jnp.exp2(s_bufs[i % 2][z] - m_ref).astype(BF16))
        if kb == nkb - 1:
            denom = acc[MLA_V:MLA_V + 1]
            ot_ref[0, :, qb * MLA_BQ:(qb + 1) * MLA_BQ] = (acc[:MLA_V] / denom).astype(BF16)
            ok = jnp.max(denom) < MLA_DENOM_LIMIT
            finite = ok if finite is None else jnp.logical_and(finite, ok)

    @pl.when(jnp.logical_not(finite))
    def _():
        _mla_attn_rescaling(q_scr, kn_ref, kpe_ref, vt_ref, ot_ref)


def _mla_attn(qn, qpe, kn, kpe, vt, batch, seq):
    bq = MLA_QSTEPS * MLA_BQ
    per_n = LANES // MLA_NOPE
    per_r = LANES // MLA_ROPE
    return pl.pallas_call(
        _mla_attn_kernel,
        grid=(batch, MLA_HEADS, seq // bq),
        in_specs=[
            pl.BlockSpec(memory_space=pltpu.SMEM),
            pl.BlockSpec((1, LANES, bq), lambda b, h, i: (b, h // per_n, i)),
            pl.BlockSpec((1, LANES, bq), lambda b, h, i: (b, h // per_r, i)),
            pl.BlockSpec((seq, LANES), lambda b, h, i: (b, h // per_n)),
            pl.BlockSpec((seq, LANES), lambda b, h, i: (b, 0)),
            pl.BlockSpec((1, 1, seq // MLA_BK, MLA_V_PAD, MLA_BK), lambda b, h, i: (b, h, 0, 0, 0)),
        ],
        out_specs=pl.BlockSpec((1, MLA_V, bq), lambda b, h, i: (b, h, i)),
        out_shape=jax.ShapeDtypeStruct((batch, MLA_HEADS * MLA_V, seq), BF16),
        scratch_shapes=[pltpu.VMEM((2 * LANES, bq), BF16),
                        pltpu.VMEM((2, MLA_BK, MLA_BQ), F32),
                        pltpu.VMEM((2, MLA_BK, MLA_BQ), F32)],
        compiler_params=pltpu.CompilerParams(
            dimension_semantics=("arbitrary", "arbitrary", "arbitrary"),
            vmem_limit_bytes=VMEM_LIMIT),
        name="mla_attn",
    )(jnp.zeros((1,), jnp.int32), qn, qpe, kn, kpe, vt)


def _rope_tables(seq):
    inv = 1.0 / (ROPE_BASE ** (jnp.arange(0, MLA_ROPE, 2, dtype=F32) / MLA_ROPE))
    ang = jnp.arange(seq, dtype=F32)[:, None] * inv[None, :]
    cos = jnp.concatenate([jnp.cos(ang), jnp.cos(ang)], axis=-1)
    sin = jnp.concatenate([jnp.sin(ang), jnp.sin(ang)], axis=-1)
    return cos, sin


def kernel(x, p, norm_g, na_w_in, na_rpb, na_w_out, mla_w_in, mla_q_norm, mla_w_qb,
           mla_kv_norm, mla_w_kvb, mla_w_out, ple_norm, ple_w_gate, ple_w_proj, final_norm):
    batch, seq, d = x.shape
    m = batch * seq
    x2 = x.reshape(m, d)
    p3 = p.reshape(p.shape[0], m, PLE_DIM)
    row = lambda v: v.reshape(1, -1).astype(F32)

    hd = NA_HEADS * NA_HEAD_DIM
    q_scale = jnp.where(jnp.arange(4 * hd) < hd, NA_HEAD_DIM ** -0.5 * LOG2E, 1.0).astype(F32)
    w_in0 = (na_w_in.reshape(d, 4 * hd) * q_scale).astype(BF16)
    qkvz = _na_inproj(x2, row(norm_g[0]), w_in0)
    o0 = _na_attn(qkvz, na_rpb.reshape(-1).astype(F32), batch, seq)
    x2 = _post_block(x2, o0, qkvz, 3, na_w_out[0].astype(BF16), row(ple_norm[0]),
                     ple_w_gate[0].astype(BF16), p3, 0,
                     ple_w_proj[0].astype(BF16), row(final_norm),
                     feature_major=False, final_norm=False, seq=seq)

    w_in1 = mla_w_in[0]
    o2 = MLA_Q_RANK + MLA_KV_RANK
    o3 = o2 + MLA_ROPE
    w_kr = w_in1[:, o2:o3]
    rep = LANES // MLA_ROPE
    wc = jnp.concatenate([w_in1[:, :o2], jnp.tile(w_kr, (1, rep))], axis=1).astype(BF16)
    wzt = w_in1[:, o3:].T.astype(BF16)

    scale = (MLA_NOPE + MLA_ROPE) ** -0.5 * LOG2E
    wq3 = mla_w_qb[0].reshape(MLA_Q_RANK, MLA_HEADS, MLA_NOPE + MLA_ROPE) * scale
    wq_n = wq3[:, :, :MLA_NOPE].reshape(MLA_Q_RANK, -1)
    wq_p = wq3[:, :, MLA_NOPE:]
    wq = jnp.concatenate([wq_n, wq_p.reshape(MLA_Q_RANK, -1)], axis=1).T.astype(BF16)

    wkv3 = mla_w_kvb[0].reshape(MLA_KV_RANK, MLA_HEADS, MLA_NOPE + MLA_V)
    wk = wkv3[:, :, :MLA_NOPE].reshape(MLA_KV_RANK, -1).astype(BF16)
    wvt = wkv3[:, :, MLA_NOPE:].reshape(MLA_KV_RANK, -1).T.astype(BF16)

    cos, sin = _rope_tables(seq)
    cosq, sinq = cos.T, sin.T
    cosk, sink = jnp.tile(cos, (1, rep)), jnp.tile(sin, (1, rep))

    qn, qpe, kn, kpe, vt, zt = _mla_inproj(
        x2, row(norm_g[1]), wc, wzt, row(mla_q_norm[0]), wq, row(mla_kv_norm[0]),
        wk, wvt, cosq, sinq, cosk, sink, batch, seq)
    ot = _mla_attn(qn, qpe, kn, kpe, vt, batch, seq)
    out = _post_block(x2, ot, zt, 0, mla_w_out[0].astype(BF16), row(ple_norm[1]),
                      ple_w_gate[1].astype(BF16), p3, 1,
                      ple_w_proj[1].astype(BF16), row(final_norm),
                      feature_major=True, final_norm=True, seq=seq)
    return out.reshape(batch, seq, d)
```
